```python
import math
import jax, jax.numpy as jnp
from jax import lax
import numpy as np

D_MODEL = 2048
BATCH = 8
SEQ = 2048
DEPTH = 1

MIX_WIDTH = D_MODEL
DIFF_WIDTH = MIX_WIDTH // 2
SB_WIDTH = MIX_WIDTH - DIFF_WIDTH
DIFF_HEADS = 4
DIFF_QK_DIM = DIFF_WIDTH // (2 * DIFF_HEADS)
DIFF_V_DIM = 2 * DIFF_QK_DIM
SB_HEADS = 8
SB_HEAD_DIM = SB_WIDTH // SB_HEADS
D_FF = ((8 * D_MODEL // 3 + 255) // 256) * 256
ROPE_THETA = 500000.0
ROPE_DIM = DIFF_QK_DIM // 4
Q_BLOCK = 128
EPS = 1e-5
DIFF_QK_COLS = DIFF_HEADS * DIFF_QK_DIM
DIFF_V_COLS = DIFF_HEADS * DIFF_V_DIM
IN_COLS = 4 * DIFF_QK_COLS + DIFF_V_COLS + 3 * SB_WIDTH

kernel_name = "hybrid_diffattn_stickbreaking_macaron"


def _rmsnorm(x, g):
    xf = x.astype(jnp.float32)
    y = xf * lax.rsqrt(jnp.mean(xf * xf, axis=-1, keepdims=True) + EPS)
    return (y * g.astype(jnp.float32)).astype(x.dtype)


def _swiglu(x, w_gate, w_up, w_down):
    return (jax.nn.silu(x @ w_gate) * (x @ w_up)) @ w_down


def _rope_tables(seq):
    pos = jnp.arange(seq, dtype=jnp.float32)
    inv_freq = ROPE_THETA ** (-jnp.arange(0, ROPE_DIM, 2, dtype=jnp.float32) / ROPE_DIM)
    ang = pos[:, None] * inv_freq[None, :]
    return jnp.cos(ang)[None, :, None, :], jnp.sin(ang)[None, :, None, :]


def _partial_rope(x, cos, sin):
    xf = x.astype(jnp.float32)
    half = ROPE_DIM // 2
    x1, x2, rest = xf[..., :half], xf[..., half:ROPE_DIM], xf[..., ROPE_DIM:]
    out = jnp.concatenate([x1 * cos - x2 * sin, x2 * cos + x1 * sin, rest], axis=-1)
    return out.astype(x.dtype)


def _softmax_map(q, k, mask, t0, t1):
    s = jnp.einsum('bqhd,bkhd->bhqk', q[:, t0:t1], k[:, :t1]).astype(jnp.float32) * (DIFF_QK_DIM ** -0.5)
    return jax.nn.softmax(jnp.where(mask, s, -jnp.inf), axis=-1)


def _diff_attention(q1, q2, k1, k2, v, lam):
    seq = q1.shape[1]
    vf = v.astype(jnp.float32)
    outs = []
    for i in range(seq // Q_BLOCK):
        t0, t1 = i * Q_BLOCK, (i + 1) * Q_BLOCK
        mask = jnp.arange(t0, t1)[:, None] >= jnp.arange(t1)[None, :]
        w = _softmax_map(q1, k1, mask, t0, t1) - lam * _softmax_map(q2, k2, mask, t0, t1)
        outs.append(jnp.einsum('bhqk,bkhd->bqhd', w, vf[:, :t1]))
    return jnp.concatenate(outs, axis=1)


def _stick_breaking(q, k, v):
    seq = q.shape[1]
    vf = v.astype(jnp.float32)
    scale = SB_HEAD_DIM ** -0.5
    outs = []
    for i in range(seq // Q_BLOCK):
        t0, t1 = i * Q_BLOCK, (i + 1) * Q_BLOCK
        mask = jnp.arange(t0, t1)[:, None] > jnp.arange(t1)[None, :]
        z = jnp.einsum('bqhd,bkhd->bhqk', q[:, t0:t1], k[:, :t1]).astype(jnp.float32) * scale
        log_1m_beta = jnp.where(mask, jax.nn.log_sigmoid(-z), 0.0)
        tail = lax.cumsum(log_1m_beta, axis=3, reverse=True) - log_1m_beta
        a = jnp.where(mask, jnp.exp(jax.nn.log_sigmoid(z) + tail), 0.0)
        outs.append(jnp.einsum('bhqk,bkhd->bqhd', a, vf[:, :t1]))
    return jnp.concatenate(outs, axis=1)


def setup_inputs(seed: int = 0) -> dict:
    key = jax.random.key(seed)
    ks = jax.random.split(key, 24)
    f32 = jnp.float32
    nrm = lambda k, shape, s: jax.random.normal(k, shape, f32) * s
    gain = lambda k, n: 1.0 + 0.01 * jax.random.normal(k, (DEPTH, n), f32)
    return {
        "x": jax.random.normal(ks[0], (BATCH, SEQ, D_MODEL), f32),
        "ffn1_norm": gain(ks[1], D_MODEL),
        "ffn1_w_gate": nrm(ks[2], (DEPTH, D_MODEL, D_FF), D_MODEL ** -0.5),
        "ffn1_w_up": nrm(ks[3], (DEPTH, D_MODEL, D_FF), D_MODEL ** -0.5),
        "ffn1_w_down": nrm(ks[4], (DEPTH, D_FF, D_MODEL), D_FF ** -0.5),
        "mix_norm": gain(ks[5], D_MODEL),
        "w_in": nrm(ks[6], (DEPTH, D_MODEL, IN_COLS), D_MODEL ** -0.5),
        "q_norm": gain(ks[7], DIFF_QK_DIM),
        "k_norm": gain(ks[8], DIFF_QK_DIM),
        "lambda_q1": nrm(ks[9], (DEPTH, DIFF_QK_DIM), 0.1),
        "lambda_k1": nrm(ks[10], (DEPTH, DIFF_QK_DIM), 0.1),
        "lambda_q2": nrm(ks[11], (DEPTH, DIFF_QK_DIM), 0.1),
        "lambda_k2": nrm(ks[12], (DEPTH, DIFF_QK_DIM), 0.1),
        "subln": gain(ks[13], DIFF_V_DIM),
        "w_out": nrm(ks[14], (DEPTH, MIX_WIDTH, D_MODEL), MIX_WIDTH ** -0.5),
        "ffn2_norm": gain(ks[15], D_MODEL),
        "ffn2_w_gate": nrm(ks[16], (DEPTH, D_MODEL, D_FF), D_MODEL ** -0.5),
        "ffn2_w_up": nrm(ks[17], (DEPTH, D_MODEL, D_FF), D_MODEL ** -0.5),
        "ffn2_w_down": nrm(ks[18], (DEPTH, D_FF, D_MODEL), D_FF ** -0.5),
        "final_norm": gain(ks[19], D_MODEL),
    }


def reference(x, ffn1_norm, ffn1_w_gate, ffn1_w_up, ffn1_w_down, mix_norm, w_in, q_norm, k_norm,
              lambda_q1, lambda_k1, lambda_q2, lambda_k2, subln, w_out,
              ffn2_norm, ffn2_w_gate, ffn2_w_up, ffn2_w_down, final_norm):
    b, s, _ = x.shape
    cos, sin = _rope_tables(s)
    for l in range(DEPTH):
        x = x + 0.5 * _swiglu(_rmsnorm(x, ffn1_norm[l]), ffn1_w_gate[l], ffn1_w_up[l], ffn1_w_down[l])

        u = _rmsnorm(x, mix_norm[l]) @ w_in[l]
        c = np.cumsum([0, DIFF_QK_COLS, DIFF_QK_COLS, DIFF_QK_COLS, DIFF_QK_COLS,
                       DIFF_V_COLS, SB_WIDTH, SB_WIDTH, SB_WIDTH]).tolist()
        q1, q2, k1, k2, dv, sq, sk, sv = [u[..., c[i]:c[i + 1]] for i in range(8)]

        def prep(t, g):
            t = t.reshape(b, s, DIFF_HEADS, DIFF_QK_DIM)
            return _partial_rope(_rmsnorm(t, g), cos, sin)
        q1, q2 = prep(q1, q_norm[l]), prep(q2, q_norm[l])
        k1, k2 = prep(k1, k_norm[l]), prep(k2, k_norm[l])
        dv = dv.reshape(b, s, DIFF_HEADS, DIFF_V_DIM)
        lambda_init = 0.8 - 0.6 * math.exp(-0.3 * l)
        lam = (jnp.exp(jnp.sum(lambda_q1[l].astype(jnp.float32) * lambda_k1[l].astype(jnp.float32)))
               - jnp.exp(jnp.sum(lambda_q2[l].astype(jnp.float32) * lambda_k2[l].astype(jnp.float32)))
               + lambda_init)
        a_out = _diff_attention(q1, q2, k1, k2, dv, lam)
        a_out = _rmsnorm(a_out, subln[l]) * (1.0 - lambda_init)
        a_out = a_out.reshape(b, s, DIFF_WIDTH).astype(x.dtype)

        sq = sq.reshape(b, s, SB_HEADS, SB_HEAD_DIM)
        sk = sk.reshape(b, s, SB_HEADS, SB_HEAD_DIM)
        sv = sv.reshape(b, s, SB_HEADS, SB_HEAD_DIM)
        b_out = _stick_breaking(sq, sk, sv).reshape(b, s, SB_WIDTH).astype(x.dtype)

        x = x + jnp.concatenate([a_out, b_out], axis=-1) @ w_out[l]

        x = x + 0.5 * _swiglu(_rmsnorm(x, ffn2_norm[l]), ffn2_w_gate[l], ffn2_w_up[l], ffn2_w_down[l])

        x = _rmsnorm(x, final_norm[l])
    return x
```

```python
import functools
import math

import jax
import jax.numpy as jnp
from jax import lax
from jax.experimental import pallas as pl
from jax.experimental.pallas import tpu as pltpu

EPS = 1e-5
ROPE_THETA = 500000.0
LAMBDA_INIT = 0.8 - 0.6 * math.exp(-0.3 * 0)

DIFF_HEADS = 4
SB_HEADS = 8
QK_DIM = 128
DIFF_V_DIM = 2 * QK_DIM
ROPE_DIM = QK_DIM // 4
LANES = 128

F32 = jnp.float32
BF16 = jnp.bfloat16

VMEM_LIMIT_BYTES = 56 * 1024 * 1024


def _params(*sem):
    return pltpu.CompilerParams(dimension_semantics=sem, vmem_limit_bytes=VMEM_LIMIT_BYTES)


def _rms(x, g):
    ms = jnp.mean(x * x, axis=-1, keepdims=True)
    return x * lax.rsqrt(ms + EPS) * g


def _dot(a, b):
    return jnp.dot(a, b, preferred_element_type=F32)


def _dot_nt(a, b):
    return lax.dot_general(a, b, (((1,), (1,)), ((), ())), preferred_element_type=F32)


def _ffn_kernel(x_ref, g_ref, wg_ref, wu_ref, wd_ref, fg_ref, o_ref, xn_ref, *, final_norm):
    f = pl.program_id(1)

    @pl.when(f == 0)
    def _():
        x = x_ref[...]
        xn_ref[...] = _rms(x, g_ref[...]).astype(BF16)
        o_ref[...] = x

    xn = xn_ref[...]
    gate = _dot(xn, wg_ref[...])
    up = _dot(xn, wu_ref[...])
    h = (gate * jax.nn.sigmoid(gate)) * (up * 0.5)
    o_ref[...] += _dot(h.astype(BF16), wd_ref[...])

    if final_norm:
        @pl.when(f == pl.num_programs(1) - 1)
        def _():
            o_ref[...] = _rms(o_ref[...], fg_ref[...])


def _ffn(x, norm_g, wg, wu, wd, final_g, *, final_norm, tm, tf):
    t, d = x.shape
    dff = wg.shape[1]
    assert t % tm == 0 and dff % tf == 0
    return pl.pallas_call(
        functools.partial(_ffn_kernel, final_norm=final_norm),
        grid=(t // tm, dff // tf),
        in_specs=[
            pl.BlockSpec((tm, d), lambda i, f: (i, 0)),
            pl.BlockSpec((1, d), lambda i, f: (0, 0)),
            pl.BlockSpec((d, tf), lambda i, f: (0, f)),
            pl.BlockSpec((d, tf), lambda i, f: (0, f)),
            pl.BlockSpec((tf, d), lambda i, f: (f, 0)),
            pl.BlockSpec((1, d), lambda i, f: (0, 0)),
        ],
        out_specs=pl.BlockSpec((tm, d), lambda i, f: (i, 0)),
        out_shape=jax.ShapeDtypeStruct((t, d), F32),
        scratch_shapes=[pltpu.VMEM((tm, d), BF16)],
        compiler_params=_params("parallel", "arbitrary"),
        name="ffn_final" if final_norm else "ffn",
    )(x, norm_g, wg, wu, wd, final_g)


def _in_proj_kernel(x_ref, g_ref, w_ref, qg_ref, kg_ref, cos_ref, sa_ref, sb_ref,
                    o_ref, xn_ref, *, tn, n_qk_tiles, sbq_lo, sbq_hi, scale):
    j = pl.program_id(1)

    @pl.when(j == 0)
    def _():
        xn_ref[...] = _rms(x_ref[...], g_ref[...]).astype(BF16)

    u = _dot(xn_ref[...], w_ref[...])

    @pl.when(j < n_qk_tiles)
    def _():
        is_q = j < n_qk_tiles // 2
        gain = jnp.where(is_q, qg_ref[...], kg_ref[...])
        post = jnp.where(is_q, scale, 1.0).astype(F32)
        cos, sa, sb = cos_ref[...], sa_ref[...], sb_ref[...]
        for c in range(tn // QK_DIM):
            t = u[:, c * QK_DIM:(c + 1) * QK_DIM]
            ms = jnp.mean(t * t, axis=-1, keepdims=True)
            t = t * (lax.rsqrt(ms + EPS) * post) * gain
            t = (t * cos + pltpu.roll(t, QK_DIM - ROPE_DIM // 2, 1) * sa
                 + pltpu.roll(t, ROPE_DIM // 2, 1) * sb)
            o_ref[:, c * QK_DIM:(c + 1) * QK_DIM] = t.astype(BF16)

    @pl.when(j >= n_qk_tiles)
    def _():
        post = jnp.where((j >= sbq_lo) & (j < sbq_hi), scale, 1.0).astype(F32)
        o_ref[...] = (u * post).astype(BF16)


def _in_proj(x, norm_g, w, qg, kg, cos_t, sa_t, sb_t, *, seq, tm, tn):
    t, d = x.shape
    n = w.shape[1]
    diff_qk_cols = DIFF_HEADS * QK_DIM
    assert t % tm == 0 and seq % tm == 0 and n % tn == 0 and diff_qk_cols % tn == 0
    n_qk_tiles = 4 * diff_qk_cols // tn
    sbq_lo = (4 * diff_qk_cols + DIFF_HEADS * DIFF_V_DIM) // tn
    sbq_hi = sbq_lo + SB_HEADS * QK_DIM // tn
    s_blocks = seq // tm
    rope_spec = pl.BlockSpec((tm, QK_DIM), lambda i, j: (i % s_blocks, 0))
    kern = functools.partial(_in_proj_kernel, tn=tn, n_qk_tiles=n_qk_tiles,
                             sbq_lo=sbq_lo, sbq_hi=sbq_hi, scale=QK_DIM ** -0.5)
    return pl.pallas_call(
        kern,
        grid=(t // tm, n // tn),
        in_specs=[
            pl.BlockSpec((tm, d), lambda i, j: (i, 0)),
            pl.BlockSpec((1, d), lambda i, j: (0, 0)),
            pl.BlockSpec((d, tn), lambda i, j: (0, j)),
            pl.BlockSpec((1, QK_DIM), lambda i, j: (0, 0)),
            pl.BlockSpec((1, QK_DIM), lambda i, j: (0, 0)),
            rope_spec, rope_spec, rope_spec,
        ],
        out_specs=pl.BlockSpec((tm, tn), lambda i, j: (i, j)),
        out_shape=jax.ShapeDtypeStruct((t, n), BF16),
        scratch_shapes=[pltpu.VMEM((tm, d), BF16)],
        compiler_params=_params("parallel", "arbitrary"),
        name="in_proj",
    )(x, norm_g, w, qg, kg, cos_t, sa_t, sb_t)


def _diff_attn_kernel(q1_ref, q2_ref, k1_ref, k2_ref, v_ref, lq1_ref, lk1_ref, lq2_ref,
                      lk2_ref, sub_ref, o_ref, m_ref, l_ref, acc_ref, *, blk):
    qi = pl.program_id(2)
    qs = (q1_ref[...], q2_ref[...])
    ks = (k1_ref, k2_ref)

    def scores(mp, j):
        kb = ks[mp][pl.ds(pl.multiple_of(j * blk, blk), blk), :]
        return _dot_nt(qs[mp], kb)

    def vblock(j):
        return v_ref[pl.ds(pl.multiple_of(j * blk, blk), blk), :]

    row = lax.broadcasted_iota(jnp.int32, (blk, blk), 0)
    col = lax.broadcasted_iota(jnp.int32, (blk, blk), 1)
    vb = vblock(qi)
    for mp in range(2):
        s = jnp.where(row >= col, scores(mp, qi), -jnp.inf)
        m = jnp.max(s, axis=-1, keepdims=True)
        p = jnp.exp(s - m)
        m_ref[mp] = m
        l_ref[mp] = jnp.sum(p, axis=-1, keepdims=True)
        acc_ref[mp] = _dot(p.astype(BF16), vb)

    def body(j, carry):
        vb = vblock(j)
        for mp in range(2):
            s = scores(mp, j)
            m_old = m_ref[mp]
            m = jnp.maximum(m_old, jnp.max(s, axis=-1, keepdims=True))
            alpha = jnp.exp(m_old - m)
            p = jnp.exp(s - m)
            m_ref[mp] = m
            l_ref[mp] = alpha * l_ref[mp] + jnp.sum(p, axis=-1, keepdims=True)
            acc_ref[mp] = alpha * acc_ref[mp] + _dot(p.astype(BF16), vb)
        return carry

    lax.fori_loop(0, qi, body, 0)

    lam = (jnp.exp(jnp.sum(lq1_ref[...] * lk1_ref[...], axis=-1, keepdims=True))
           - jnp.exp(jnp.sum(lq2_ref[...] * lk2_ref[...], axis=-1, keepdims=True))
           + LAMBDA_INIT)
    o = acc_ref[0] / l_ref[0] - lam * (acc_ref[1] / l_ref[1])
    o_ref[...] = (_rms(o, sub_ref[...]) * (1.0 - LAMBDA_INIT)).astype(o_ref.dtype)


def _diff_attn(u, lq1, lk1, lq2, lk2, subln, *, batch, seq, blk):
    t = u.shape[0]
    nq = seq // blk
    h_ = DIFF_HEADS
    v_off = 4 * h_ * QK_DIM // DIFF_V_DIM
    lam_spec = pl.BlockSpec((1, QK_DIM), lambda b, h, q: (0, 0))
    return pl.pallas_call(
        functools.partial(_diff_attn_kernel, blk=blk),
        grid=(batch, h_, nq),
        in_specs=[
            pl.BlockSpec((blk, QK_DIM), lambda b, h, q: (b * nq + q, h)),
            pl.BlockSpec((blk, QK_DIM), lambda b, h, q: (b * nq + q, h_ + h)),
            pl.BlockSpec((seq, QK_DIM), lambda b, h, q: (b, 2 * h_ + h)),
            pl.BlockSpec((seq, QK_DIM), lambda b, h, q: (b, 3 * h_ + h)),
            pl.BlockSpec((seq, DIFF_V_DIM), lambda b, h, q: (b, v_off + h)),
            lam_spec, lam_spec, lam_spec, lam_spec,
            pl.BlockSpec((1, DIFF_V_DIM), lambda b, h, q: (0, 0)),
        ],
        out_specs=pl.BlockSpec((blk, DIFF_V_DIM), lambda b, h, q: (b * nq + q, h)),
        out_shape=jax.ShapeDtypeStruct((t, h_ * DIFF_V_DIM), BF16),
        scratch_shapes=[
            pltpu.VMEM((2, blk, 1), F32),
            pltpu.VMEM((2, blk, 1), F32),
            pltpu.VMEM((2, blk, DIFF_V_DIM), F32),
        ],
        compiler_params=_params("parallel", "parallel", "arbitrary"),
        name="diff_attn",
    )(u, u, u, u, u, lq1, lk1, lq2, lk2, subln)


def _sb_attn_kernel(q_ref, k_ref, v_ref, o_ref, vt_ref, c_ref, acc_ref, *, blk):
    qi = pl.program_id(2)

    @pl.when(qi == 0)
    def _():
        vt_ref[...] = v_ref[...].astype(F32).T.astype(BF16)

    q = q_ref[...]
    krow = lax.broadcasted_iota(jnp.int32, (blk, blk), 0)
    kcol = lax.broadcasted_iota(jnp.int32, (blk, blk), 1)
    upper = (kcol > krow).astype(BF16)
    upper2 = jnp.concatenate([upper, upper], axis=1)

    def block(j, mask):
        start = pl.multiple_of(j * blk, blk)
        z = _dot_nt(k_ref[pl.ds(start, blk), :], q)
        sp = jnp.maximum(z, 0.0) + jnp.log(1.0 + jnp.exp(-jnp.abs(z)))
        if mask is not None:
            sp = jnp.where(mask, sp, 0.0)
        hi = sp.astype(BF16)
        lo = (sp - hi.astype(F32)).astype(BF16)
        tail = _dot(upper2, jnp.concatenate([hi, lo], axis=0))
        c = c_ref[...]
        a = jnp.exp(z - sp - tail - c)
        if mask is not None:
            a = jnp.where(mask, a, 0.0)
        acc = _dot(vt_ref[:, pl.ds(start, blk)], a.astype(BF16))
        c_ref[...] = c + jnp.sum(sp, axis=0, keepdims=True)
        return acc

    c_ref[...] = jnp.zeros_like(c_ref)
    acc_ref[...] = block(qi, krow < kcol)

    def body(jj, carry):
        acc_ref[...] += block(qi - 1 - jj, None)
        return carry

    lax.fori_loop(0, qi, body, 0)
    o_ref[...] = acc_ref[...].T.astype(o_ref.dtype)


def _sb_attn(u, *, batch, seq, blk):
    t = u.shape[0]
    nq = seq // blk
    h_ = SB_HEADS
    off = (4 * DIFF_HEADS * QK_DIM + DIFF_HEADS * DIFF_V_DIM) // QK_DIM
    return pl.pallas_call(
        functools.partial(_sb_attn_kernel, blk=blk),
        grid=(batch, h_, nq),
        in_specs=[
            pl.BlockSpec((blk, QK_DIM), lambda b, h, q: (b * nq + q, off + h)),
            pl.BlockSpec((seq, QK_DIM), lambda b, h, q: (b, off + h_ + h)),
            pl.BlockSpec((seq, QK_DIM), lambda b, h, q: (b, off + 2 * h_ + h)),
        ],
        out_specs=pl.BlockSpec((blk, QK_DIM), lambda b, h, q: (b * nq + q, h)),
        out_shape=jax.ShapeDtypeStruct((t, h_ * QK_DIM), BF16),
        scratch_shapes=[
            pltpu.VMEM((QK_DIM, seq), BF16),
            pltpu.VMEM((1, blk), F32),
            pltpu.VMEM((QK_DIM, blk), F32),
        ],
        compiler_params=_params("parallel", "parallel", "arbitrary"),
        name="sb_attn",
    )(u, u, u)


def _out_proj_kernel(x_ref, a_ref, b_ref, wa_ref, wb_ref, o_ref):
    o_ref[...] = x_ref[...] + _dot(a_ref[...], wa_ref[...]) + _dot(b_ref[...], wb_ref[...])


def _out_proj(x, a, b, w, *, tm):
    t, d = x.shape
    wa_rows, wb_rows = a.shape[1], b.shape[1]
    assert t % tm == 0 and wa_rows == wb_rows and w.shape[0] == wa_rows + wb_rows
    return pl.pallas_call(
        _out_proj_kernel,
        grid=(t // tm,),
        in_specs=[
            pl.BlockSpec((tm, d), lambda i: (i, 0)),
            pl.BlockSpec((tm, wa_rows), lambda i: (i, 0)),
            pl.BlockSpec((tm, wb_rows), lambda i: (i, 0)),
            pl.BlockSpec((wa_rows, d), lambda i: (0, 0)),
            pl.BlockSpec((wb_rows, d), lambda i: (1, 0)),
        ],
        out_specs=pl.BlockSpec((tm, d), lambda i: (i, 0)),
        out_shape=jax.ShapeDtypeStruct((t, d), F32),
        compiler_params=_params("parallel"),
        name="out_proj",
    )(x, a, b, w, w)


def _rope_tables(seq):
    half = ROPE_DIM // 2
    pos = jnp.arange(seq, dtype=F32)
    inv_freq = ROPE_THETA ** (-jnp.arange(0, ROPE_DIM, 2, dtype=F32) / ROPE_DIM)
    ang = pos[:, None] * inv_freq[None, :]
    cos, sin = jnp.cos(ang), jnp.sin(ang)
    ones = jnp.ones((seq, QK_DIM - ROPE_DIM), F32)
    zeros_h = jnp.zeros((seq, half), F32)
    zeros_r = jnp.zeros((seq, QK_DIM - ROPE_DIM), F32)
    cos_t = jnp.concatenate([cos, cos, ones], axis=1)
    sa_t = jnp.concatenate([-sin, zeros_h, zeros_r], axis=1)
    sb_t = jnp.concatenate([zeros_h, sin, zeros_r], axis=1)
    return cos_t, sa_t, sb_t


def _pick(n, pref):
    for c in pref:
        if n % c == 0:
            return c
    return n


def kernel(x, ffn1_norm, ffn1_w_gate, ffn1_w_up, ffn1_w_down, mix_norm, w_in, q_norm, k_norm,
           lambda_q1, lambda_k1, lambda_q2, lambda_k2, subln, w_out,
           ffn2_norm, ffn2_w_gate, ffn2_w_up, ffn2_w_down, final_norm):
    batch, seq, d = x.shape
    assert ffn1_norm.shape[0] == 1, "single-layer block"
    t = batch * seq
    l = 0
    dff = ffn1_w_gate.shape[-1]
    tm = _pick(t, (512, 256, 128))
    tf = _pick(dff, (512, 256, 128))
    blk = _pick(seq, (256, 128))
    bf = lambda w: w[l].astype(BF16)
    row = lambda g: g[l].astype(F32)[None, :]

    xt = x.reshape(t, d)
    x1 = _ffn(xt, row(ffn1_norm), bf(ffn1_w_gate), bf(ffn1_w_up), bf(ffn1_w_down),
              row(final_norm), final_norm=False, tm=tm, tf=tf)

    cos_t, sa_t, sb_t = _rope_tables(seq)
    u = _in_proj(x1, row(mix_norm), bf(w_in), row(q_norm), row(k_norm), cos_t, sa_t, sb_t,
                 seq=seq, tm=_pick(seq, (1024, 512, 256, 128)), tn=512)

    a = _diff_attn(u, row(lambda_q1), row(lambda_k1), row(lambda_q2), row(lambda_k2),
                   row(subln), batch=batch, seq=seq, blk=blk)
    b = _sb_attn(u, batch=batch, seq=seq, blk=blk)

    x2 = _out_proj(x1, a, b, bf(w_out), tm=tm)

    out = _ffn(x2, row(ffn2_norm), bf(ffn2_w_gate), bf(ffn2_w_up), bf(ffn2_w_down),
               row(final_norm), final_norm=True, tm=tm, tf=tf)
    return out.reshape(batch, seq, d)
```

```python
import functools
import math

import jax
import jax.numpy as jnp
from jax import lax
from jax.experimental import pallas as pl
from jax.experimental.pallas import tpu as pltpu

EPS = 1e-5
ROPE_THETA = 500000.0
LAMBDA_INIT = 0.8 - 0.6 * math.exp(-0.3 * 0)
LOG2E = math.log2(math.e)
MASKED_LOGIT = 1e30

DIFF_HEADS = 4
SB_HEADS = 8
QK_DIM = 128
DIFF_V_DIM = 2 * QK_DIM
ROPE_DIM = QK_DIM // 4

F32 = jnp.float32
BF16 = jnp.bfloat16

VMEM_LIMIT_BYTES = 56 * 1024 * 1024


def _params(*sem):
    return pltpu.CompilerParams(dimension_semantics=sem, vmem_limit_bytes=VMEM_LIMIT_BYTES)


def _rms(x, g):
    ms = jnp.mean(x * x, axis=-1, keepdims=True)
    return x * lax.rsqrt(ms + EPS) * g


def _dot(a, b):
    return jnp.dot(a, b, preferred_element_type=F32)


def _dot_nt(a, b):
    return lax.dot_general(a, b, (((1,), (1,)), ((), ())), preferred_element_type=F32)


def _ffn_kernel(x_ref, g_ref, wg_ref, wu_ref, wd_ref, fg_ref, o_ref, xn_ref, *, final_norm):
    f = pl.program_id(1)

    @pl.when(f == 0)
    def _():
        x = x_ref[...]
        xn_ref[...] = _rms(x, g_ref[...]).astype(BF16)
        o_ref[...] = x

    xn = xn_ref[...]
    gate = _dot(xn, wg_ref[...])
    up = _dot(xn, wu_ref[...])
    h = (gate * jax.nn.sigmoid(gate)) * (up * 0.5)
    o_ref[...] += _dot(h.astype(BF16), wd_ref[...])

    if final_norm:
        @pl.when(f == pl.num_programs(1) - 1)
        def _():
            o_ref[...] = _rms(o_ref[...], fg_ref[...])


def _ffn(x, norm_g, wg, wu, wd, final_g, *, final_norm, tm, tf):
    t, d = x.shape
    dff = wg.shape[1]
    assert t % tm == 0 and dff % tf == 0
    return pl.pallas_call(
        functools.partial(_ffn_kernel, final_norm=final_norm),
        grid=(t // tm, dff // tf),
        in_specs=[
            pl.BlockSpec((tm, d), lambda i, f: (i, 0)),
            pl.BlockSpec((1, d), lambda i, f: (0, 0)),
            pl.BlockSpec((d, tf), lambda i, f: (0, f)),
            pl.BlockSpec((d, tf), lambda i, f: (0, f)),
            pl.BlockSpec((tf, d), lambda i, f: (f, 0)),
            pl.BlockSpec((1, d), lambda i, f: (0, 0)),
        ],
        out_specs=pl.BlockSpec((tm, d), lambda i, f: (i, 0)),
        out_shape=jax.ShapeDtypeStruct((t, d), F32),
        scratch_shapes=[pltpu.VMEM((tm, d), BF16)],
        compiler_params=_params("parallel", "arbitrary"),
        name="ffn_final" if final_norm else "ffn",
    )(x, norm_g, wg, wu, wd, final_g)


def _in_proj_kernel(x_ref, g_ref, w_ref, qg_ref, kg_ref, cos_ref, sa_ref, sb_ref,
                    o_ref, xn_ref, *, tn, n_qk_tiles, sbq_lo, sbq_hi, scale):
    j = pl.program_id(1)

    @pl.when(j == 0)
    def _():
        xn_ref[...] = _rms(x_ref[...], g_ref[...]).astype(BF16)

    u = _dot(xn_ref[...], w_ref[...])

    @pl.when(j < n_qk_tiles)
    def _():
        is_q = j < n_qk_tiles // 2
        gain = jnp.where(is_q, qg_ref[...], kg_ref[...])
        post = jnp.where(is_q, scale, 1.0).astype(F32)
        cos, sa, sb = cos_ref[...], sa_ref[...], sb_ref[...]
        for c in range(tn // QK_DIM):
            t = u[:, c * QK_DIM:(c + 1) * QK_DIM]
            ms = jnp.mean(t * t, axis=-1, keepdims=True)
            t = t * (lax.rsqrt(ms + EPS) * post) * gain
            t = (t * cos + pltpu.roll(t, QK_DIM - ROPE_DIM // 2, 1) * sa
                 + pltpu.roll(t, ROPE_DIM // 2, 1) * sb)
            o_ref[:, c * QK_DIM:(c + 1) * QK_DIM] = t.astype(BF16)

    @pl.when(j >= n_qk_tiles)
    def _():
        post = jnp.where((j >= sbq_lo) & (j < sbq_hi), scale, 1.0).astype(F32)
        o_ref[...] = (u * post).astype(BF16)


def _in_proj(x, norm_g, w, qg, kg, cos_t, sa_t, sb_t, *, seq, tm, tn):
    t, d = x.shape
    n = w.shape[1]
    diff_qk_cols = DIFF_HEADS * QK_DIM
    assert t % tm == 0 and seq % tm == 0 and n % tn == 0 and diff_qk_cols % tn == 0
    n_qk_tiles = 4 * diff_qk_cols // tn
    sbq_lo = (4 * diff_qk_cols + DIFF_HEADS * DIFF_V_DIM) // tn
    sbq_hi = sbq_lo + SB_HEADS * QK_DIM // tn
    s_blocks = seq // tm
    rope_spec = pl.BlockSpec((tm, QK_DIM), lambda i, j: (i % s_blocks, 0))
    kern = functools.partial(_in_proj_kernel, tn=tn, n_qk_tiles=n_qk_tiles,
                             sbq_lo=sbq_lo, sbq_hi=sbq_hi, scale=QK_DIM ** -0.5 * LOG2E)
    return pl.pallas_call(
        kern,
        grid=(t // tm, n // tn),
        in_specs=[
            pl.BlockSpec((tm, d), lambda i, j: (i, 0)),
            pl.BlockSpec((1, d), lambda i, j: (0, 0)),
            pl.BlockSpec((d, tn), lambda i, j: (0, j)),
            pl.BlockSpec((1, QK_DIM), lambda i, j: (0, 0)),
            pl.BlockSpec((1, QK_DIM), lambda i, j: (0, 0)),
            rope_spec, rope_spec, rope_spec,
        ],
        out_specs=pl.BlockSpec((tm, tn), lambda i, j: (i, j)),
        out_shape=jax.ShapeDtypeStruct((t, n), BF16),
        scratch_shapes=[pltpu.VMEM((tm, d), BF16)],
        compiler_params=_params("parallel", "arbitrary"),
        name="in_proj",
    )(x, norm_g, w, qg, kg, cos_t, sa_t, sb_t)


def _diff_attn_kernel(q1_ref, q2_ref, k1_ref, k2_ref, v_ref, lq1_ref, lk1_ref, lq2_ref,
                      lk2_ref, sub_ref, o_ref, vt_ref, sa_ref, sb_ref, m_ref, l_ref, acc_ref, *, blk, heads):
    qi = pl.program_id(2)

    @pl.when(qi == 0)
    def _():
        vt_ref[...] = v_ref[...].astype(F32).T.astype(BF16)

    q_refs = (q1_ref, q2_ref)
    k_refs = (k1_ref, k2_ref)
    krow = lax.broadcasted_iota(jnp.int32, (blk, blk), 0)
    kcol = lax.broadcasted_iota(jnp.int32, (blk, blk), 1)

    chains = [(g, mp) for g in range(heads) for mp in range(2)]

    def scores(j, s_ref):
        start = pl.multiple_of(j * blk, blk)
        for idx, (g, mp) in enumerate(chains):
            hd = slice(g * QK_DIM, (g + 1) * QK_DIM)
            s_ref[idx] = _dot_nt(k_refs[mp][pl.ds(start, blk), hd], q_refs[mp][:, hd])

    def consume(j, s_ref, diag):
        start = pl.multiple_of(j * blk, blk)
        probs, alphas = [], []
        for idx in range(len(chains)):
            s = s_ref[idx]
            if diag:
                s = jnp.where(krow <= kcol, s, -jnp.inf)
            m_old = m_ref[idx]
            m = jnp.maximum(m_old, jnp.max(s, axis=0, keepdims=True))
            alpha = jnp.exp2(m_old - m)
            p = jnp.exp2(s - m)
            l_ref[idx] = alpha * l_ref[idx] + jnp.sum(p, axis=0, keepdims=True)
            m_ref[idx] = m
            alphas.append(alpha)
            probs.append(p.astype(BF16))
        for idx, (g, mp) in enumerate(chains):
            vt = vt_ref[g * DIFF_V_DIM:(g + 1) * DIFF_V_DIM, pl.ds(start, blk)]
            acc_ref[idx] = alphas[idx] * acc_ref[idx] + _dot(vt, probs[idx])

    m_ref[...] = jnp.full(m_ref.shape, -jnp.inf, F32)
    l_ref[...] = jnp.zeros(l_ref.shape, F32)
    acc_ref[...] = jnp.zeros(acc_ref.shape, F32)

    scores(0, sa_ref)
    pairs = lax.shift_right_logical(qi, 1)

    def body(p, carry):
        j = 2 * p
        scores(j + 1, sb_ref)
        consume(j, sa_ref, False)
        scores(j + 2, sa_ref)
        consume(j + 1, sb_ref, False)
        return carry

    lax.fori_loop(0, pairs, body, 0)

    @pl.when(qi == 2 * pairs)
    def _():
        consume(qi, sa_ref, True)

    @pl.when(qi != 2 * pairs)
    def _():
        scores(qi, sb_ref)
        consume(qi - 1, sa_ref, False)
        consume(qi, sb_ref, True)

    lam = (jnp.exp(jnp.sum(lq1_ref[...] * lk1_ref[...], axis=-1, keepdims=True))
           - jnp.exp(jnp.sum(lq2_ref[...] * lk2_ref[...], axis=-1, keepdims=True))
           + LAMBDA_INIT)
    for g in range(heads):
        o = (acc_ref[2 * g] * (1.0 / l_ref[2 * g])
             - acc_ref[2 * g + 1] * (lam / l_ref[2 * g + 1]))
        ms = jnp.mean(o * o, axis=0, keepdims=True)
        o = (o * lax.rsqrt(ms + EPS)).T * sub_ref[...] * (1.0 - LAMBDA_INIT)
        o_ref[:, g * DIFF_V_DIM:(g + 1) * DIFF_V_DIM] = o.astype(o_ref.dtype)


def _diff_attn(u, lq1, lk1, lq2, lk2, subln, *, batch, seq, blk, heads):
    t = u.shape[0]
    nq = seq // blk
    ng = DIFF_HEADS // heads
    assert DIFF_HEADS % heads == 0 and seq % blk == 0
    v_off = 4 * DIFF_HEADS * QK_DIM // (heads * DIFF_V_DIM)
    lam_spec = pl.BlockSpec((1, QK_DIM), lambda b, h, q: (0, 0))
    return pl.pallas_call(
        functools.partial(_diff_attn_kernel, blk=blk, heads=heads),
        grid=(batch, ng, nq),
        in_specs=[
            pl.BlockSpec((blk, heads * QK_DIM), lambda b, h, q: (b * nq + q, h)),
            pl.BlockSpec((blk, heads * QK_DIM), lambda b, h, q: (b * nq + q, ng + h)),
            pl.BlockSpec((seq, heads * QK_DIM), lambda b, h, q: (b, 2 * ng + h)),
            pl.BlockSpec((seq, heads * QK_DIM), lambda b, h, q: (b, 3 * ng + h)),
            pl.BlockSpec((seq, heads * DIFF_V_DIM), lambda b, h, q: (b, v_off + h)),
            lam_spec, lam_spec, lam_spec, lam_spec,
            pl.BlockSpec((1, DIFF_V_DIM), lambda b, h, q: (0, 0)),
        ],
        out_specs=pl.BlockSpec((blk, heads * DIFF_V_DIM), lambda b, h, q: (b * nq + q, h)),
        out_shape=jax.ShapeDtypeStruct((t, DIFF_HEADS * DIFF_V_DIM), BF16),
        scratch_shapes=[
            pltpu.VMEM((heads * DIFF_V_DIM, seq), BF16),
            pltpu.VMEM((2 * heads, blk, blk), F32),
            pltpu.VMEM((2 * heads, blk, blk), F32),
            pltpu.VMEM((2 * heads, 1, blk), F32),
            pltpu.VMEM((2 * heads, 1, blk), F32),
            pltpu.VMEM((2 * heads, DIFF_V_DIM, blk), F32),
        ],
        compiler_params=_params("parallel", "parallel", "arbitrary"),
        name="diff_attn",
    )(u, u, u, u, u, lq1, lk1, lq2, lk2, subln)


def _sb_attn_kernel(q_ref, k_ref, v_ref, o_ref, vt_ref, up_ref, sa_ref, sb_ref, c_ref, acc_ref,
                    *, blk, heads):
    qi = pl.program_id(2)
    krow = lax.broadcasted_iota(jnp.int32, (blk, blk), 0)
    kcol = lax.broadcasted_iota(jnp.int32, (blk, blk), 1)

    @pl.when(qi == 0)
    def _():
        vt_ref[...] = v_ref[...].astype(F32).T.astype(BF16)
        upper = (kcol >= krow).astype(BF16)
        up_ref[...] = jnp.concatenate([upper, upper], axis=1)

    heads_hd = [slice(g * QK_DIM, (g + 1) * QK_DIM) for g in range(heads)]

    def scores(j, s_ref):
        start = pl.multiple_of(j * blk, blk)
        for g, hd in enumerate(heads_hd):
            s_ref[g] = _dot_nt(k_ref[pl.ds(start, blk), hd], q_ref[:, hd])

    def consume(j, s_ref, diag):
        start = pl.multiple_of(j * blk, blk)
        logits, hilo = [], []
        for g in range(heads):
            z = s_ref[g]
            if diag:
                z = jnp.where(krow < kcol, z, -MASKED_LOGIT)
            sp = jnp.maximum(z, 0.0) + jnp.log2(1.0 + jnp.exp2(-jnp.abs(z)))
            if diag:
                c_ref[g] = jnp.sum(sp, axis=0, keepdims=True)
            else:
                shift = c_ref[g]
                c_ref[g] = shift + jnp.sum(sp, axis=0, keepdims=True)
                z = z - shift
            logits.append(z)
            hi = sp.astype(BF16)
            lo = (sp - hi.astype(F32)).astype(BF16)
            hilo.append(jnp.concatenate([hi, lo], axis=0))
        tails = [_dot(up_ref[...], x) for x in hilo]
        weights = [jnp.exp2(logits[g] - tails[g]).astype(BF16) for g in range(heads)]
        for g, hd in enumerate(heads_hd):
            pv = _dot(vt_ref[hd, pl.ds(start, blk)], weights[g])
            if diag:
                acc_ref[g] = pv
            else:
                acc_ref[g] += pv

    def block(i):
        return jnp.maximum(qi - i, 0)

    scores(qi, sa_ref)
    scores(block(1), sb_ref)
    consume(qi, sa_ref, True)
    pairs = lax.shift_right_logical(qi, 1)

    def body(p, carry):
        i = 1 + 2 * p
        scores(block(i + 1), sa_ref)
        consume(block(i), sb_ref, False)
        scores(block(i + 2), sb_ref)
        consume(block(i + 1), sa_ref, False)
        return carry

    lax.fori_loop(0, pairs, body, 0)

    @pl.when(qi != 2 * pairs)
    def _():
        consume(0, sb_ref, False)

    for g in range(heads):
        o_ref[:, g * QK_DIM:(g + 1) * QK_DIM] = acc_ref[g].T.astype(o_ref.dtype)


def _sb_attn(u, *, batch, seq, blk, heads):
    t = u.shape[0]
    nq = seq // blk
    ng = SB_HEADS // heads
    assert SB_HEADS % heads == 0 and seq % blk == 0
    w = heads * QK_DIM
    off = (4 * DIFF_HEADS * QK_DIM + DIFF_HEADS * DIFF_V_DIM) // w
    return pl.pallas_call(
        functools.partial(_sb_attn_kernel, blk=blk, heads=heads),
        grid=(batch, ng, nq),
        in_specs=[
            pl.BlockSpec((blk, w), lambda b, h, q: (b * nq + q, off + h)),
            pl.BlockSpec((seq, w), lambda b, h, q: (b, off + ng + h)),
            pl.BlockSpec((seq, w), lambda b, h, q: (b, off + 2 * ng + h)),
        ],
        out_specs=pl.BlockSpec((blk, w), lambda b, h, q: (b * nq + q, h)),
        out_shape=jax.ShapeDtypeStruct((t, SB_HEADS * QK_DIM), BF16),
        scratch_shapes=[
            pltpu.VMEM((w, seq), BF16),
            pltpu.VMEM((blk, 2 * blk), BF16),
            pltpu.VMEM((heads, blk, blk), F32),
            pltpu.VMEM((heads, blk, blk), F32),
            pltpu.VMEM((heads, 1, blk), F32),
            pltpu.VMEM((heads, QK_DIM, blk), F32),
        ],
        compiler_params=_params("parallel", "parallel", "arbitrary"),
        name="sb_attn",
    )(u, u, u)


def _out_proj_kernel(x_ref, a_ref, b_ref, wa_ref, wb_ref, o_ref):
    o_ref[...] = x_ref[...] + _dot(a_ref[...], wa_ref[...]) + _dot(b_ref[...], wb_ref[...])


def _out_proj(x, a, b, w, *, tm):
    t, d = x.shape
    wa_rows, wb_rows = a.shape[1], b.shape[1]
    assert t % tm == 0 and wa_rows == wb_rows and w.shape[0] == wa_rows + wb_rows
    return pl.pallas_call(
        _out_proj_kernel,
        grid=(t // tm,),
        in_specs=[
            pl.BlockSpec((tm, d), lambda i: (i, 0)),
            pl.BlockSpec((tm, wa_rows), lambda i: (i, 0)),
            pl.BlockSpec((tm, wb_rows), lambda i: (i, 0)),
            pl.BlockSpec((wa_rows, d), lambda i: (0, 0)),
            pl.BlockSpec((wb_rows, d), lambda i: (1, 0)),
        ],
        out_specs=pl.BlockSpec((tm, d), lambda i: (i, 0)),
        out_shape=jax.ShapeDtypeStruct((t, d), F32),
        compiler_params=_params("parallel"),
        name="out_proj",
    )(x, a, b, w, w)


def _rope_tables(seq):
    half = ROPE_DIM // 2
    pos = jnp.arange(seq, dtype=F32)
    inv_freq = ROPE_THETA ** (-jnp.arange(0, ROPE_DIM, 2, dtype=F32) / ROPE_DIM)
    ang = pos[:, None] * inv_freq[None, :]
    cos, sin = jnp.cos(ang), jnp.sin(ang)
    ones = jnp.ones((seq, QK_DIM - ROPE_DIM), F32)
    zeros_h = jnp.zeros((seq, half), F32)
    zeros_r = jnp.zeros((seq, QK_DIM - ROPE_DIM), F32)
    cos_t = jnp.concatenate([cos, cos, ones], axis=1)
    sa_t = jnp.concatenate([-sin, zeros_h, zeros_r], axis=1)
    sb_t = jnp.concatenate([zeros_h, sin, zeros_r], axis=1)
    return cos_t, sa_t, sb_t


def _pick(n, pref):
    for c in pref:
        if n % c == 0:
            return c
    return n


def kernel(x, ffn1_norm, ffn1_w_gate, ffn1_w_up, ffn1_w_down, mix_norm, w_in, q_norm, k_norm,
           lambda_q1, lambda_k1, lambda_q2, lambda_k2, subln, w_out,
           ffn2_norm, ffn2_w_gate, ffn2_w_up, ffn2_w_down, final_norm):
    batch, seq, d = x.shape
    assert ffn1_norm.shape[0] == 1, "single-layer block"
    t = batch * seq
    l = 0
    dff = ffn1_w_gate.shape[-1]
    tm = _pick(t, (512, 256, 128))
    tf = _pick(dff, (512, 256, 128))
    blk = _pick(seq, (256, 128))
    bf = lambda w: w[l].astype(BF16)
    row = lambda g: g[l].astype(F32)[None, :]

    xt = x.reshape(t, d)
    x1 = _ffn(xt, row(ffn1_norm), bf(ffn1_w_gate), bf(ffn1_w_up), bf(ffn1_w_down),
              row(final_norm), final_norm=False, tm=tm, tf=tf)

    cos_t, sa_t, sb_t = _rope_tables(seq)
    u = _in_proj(x1, row(mix_norm), bf(w_in), row(q_norm), row(k_norm), cos_t, sa_t, sb_t,
                 seq=seq, tm=_pick(seq, (1024, 512, 256, 128)), tn=512)

    a = _diff_attn(u, row(lambda_q1), row(lambda_k1), row(lambda_q2), row(lambda_k2),
                   row(subln), batch=batch, seq=seq, blk=blk, heads=2)
    b = _sb_attn(u, batch=batch, seq=seq, blk=blk, heads=4)

    x2 = _out_proj(x1, a, b, bf(w_out), tm=tm)

    out = _ffn(x2, row(ffn2_norm), bf(ffn2_w_gate), bf(ffn2_w_up), bf(ffn2_w_down),
               row(final_norm), final_norm=True, tm=tm, tf=tf)
    return out.reshape(batch, seq, d)
```

```python
import functools
import math

import jax
import jax.numpy as jnp
from jax import lax
from jax.experimental import pallas as pl
from jax.experimental.pallas import tpu as pltpu

EPS = 1e-5
ROPE_THETA = 500000.0
LAMBDA_INIT = 0.8 - 0.6 * math.exp(-0.3 * 0)
LOG2E = math.log2(math.e)
MASKED_LOGIT = 1e30

DIFF_HEADS = 4
SB_HEADS = 8
QK_DIM = 128
DIFF_V_DIM = 2 * QK_DIM
ROPE_DIM = QK_DIM // 4

F32 = jnp.float32
BF16 = jnp.bfloat16

VMEM_LIMIT_BYTES = 56 * 1024 * 1024
ROW_CHUNK = 256


def _params(*sem):
    return pltpu.CompilerParams(dimension_semantics=sem, vmem_limit_bytes=VMEM_LIMIT_BYTES)


def _rms(x, g):
    ms = jnp.mean(x * x, axis=-1, keepdims=True)
    return x * lax.rsqrt(ms + EPS) * g


def _dot(a, b):
    return jnp.dot(a, b, preferred_element_type=F32)


def _dot_nt(a, b):
    return lax.dot_general(a, b, (((1,), (1,)), ((), ())), preferred_element_type=F32)


def _ffn_kernel(x_ref, g_ref, wg_ref, wu_ref, wd_ref, fg_ref, o_ref, xn_ref, *, final_norm):
    f = pl.program_id(1)

    @pl.when(f == 0)
    def _():
        x = x_ref[...]
        xn_ref[...] = _rms(x, g_ref[...]).astype(BF16)
        o_ref[...] = x

    xn = xn_ref[...]
    gate = _dot(xn, wg_ref[...].astype(BF16))
    up = _dot(xn, wu_ref[...].astype(BF16))
    h = (gate * jax.nn.sigmoid(gate)) * (up * 0.5)
    o_ref[...] += _dot(h.astype(BF16), wd_ref[...].astype(BF16))

    if final_norm:
        @pl.when(f == pl.num_programs(1) - 1)
        def _():
            o_ref[...] = _rms(o_ref[...], fg_ref[...])


def _ffn(x, norm_g, wg, wu, wd, final_g, *, final_norm, tm, tf):
    t, d = x.shape
    dff = wg.shape[1]
    assert t % tm == 0 and dff % tf == 0
    return pl.pallas_call(
        functools.partial(_ffn_kernel, final_norm=final_norm),
        grid=(t // tm, dff // tf),
        in_specs=[
            pl.BlockSpec((tm, d), lambda i, f: (i, 0)),
            pl.BlockSpec((1, d), lambda i, f: (0, 0)),
            pl.BlockSpec((d, tf), lambda i, f: (0, f)),
            pl.BlockSpec((d, tf), lambda i, f: (0, f)),
            pl.BlockSpec((tf, d), lambda i, f: (f, 0)),
            pl.BlockSpec((1, d), lambda i, f: (0, 0)),
        ],
        out_specs=pl.BlockSpec((tm, d), lambda i, f: (i, 0)),
        out_shape=jax.ShapeDtypeStruct((t, d), F32),
        scratch_shapes=[pltpu.VMEM((tm, d), BF16)],
        compiler_params=_params("parallel", "arbitrary"),
        name="ffn_final" if final_norm else "ffn",
    )(x, norm_g, wg, wu, wd, final_g)


def _in_proj_kernel(x_ref, g_ref, w_ref, qg_ref, kg_ref, cos_ref, sa_ref, sb_ref,
                    o_ref, xn_ref, *, tn, n_qk_tiles, sbq_lo, sbq_hi, scale):
    j = pl.program_id(1)

    @pl.when(j == 0)
    def _():
        xn_ref[...] = _rms(x_ref[...], g_ref[...]).astype(BF16)

    tm = o_ref.shape[0]
    chunk = min(tm, ROW_CHUNK)

    @pl.when(j < n_qk_tiles)
    def _():
        w = w_ref[...].astype(BF16)
        is_q = j < n_qk_tiles // 2
        gain = jnp.where(is_q, qg_ref[...], kg_ref[...])
        post = jnp.where(is_q, scale, 1.0).astype(F32)
        for r in range(tm // chunk):
            rows = slice(r * chunk, (r + 1) * chunk)
            u = _dot(xn_ref[rows, :], w)
            cos, sa, sb = cos_ref[rows, :], sa_ref[rows, :], sb_ref[rows, :]
            for c in range(tn // QK_DIM):
                t = u[:, c * QK_DIM:(c + 1) * QK_DIM]
                ms = jnp.mean(t * t, axis=-1, keepdims=True)
                t = t * (lax.rsqrt(ms + EPS) * post) * gain
                t = (t * cos + pltpu.roll(t, QK_DIM - ROPE_DIM // 2, 1) * sa
                     + pltpu.roll(t, ROPE_DIM // 2, 1) * sb)
                o_ref[rows, c * QK_DIM:(c + 1) * QK_DIM] = t.astype(BF16)

    @pl.when(j >= n_qk_tiles)
    def _():
        post = jnp.where((j >= sbq_lo) & (j < sbq_hi), scale, 1.0).astype(F32)
        u = _dot(xn_ref[...], w_ref[...].astype(BF16))
        o_ref[...] = (u * post).astype(BF16)


def _in_proj(x, norm_g, w, qg, kg, cos_t, sa_t, sb_t, *, seq, tm, tn):
    t, d = x.shape
    n = w.shape[1]
    diff_qk_cols = DIFF_HEADS * QK_DIM
    assert t % tm == 0 and seq % tm == 0 and n % tn == 0 and diff_qk_cols % tn == 0
    n_qk_tiles = 4 * diff_qk_cols // tn
    sbq_lo = (4 * diff_qk_cols + DIFF_HEADS * DIFF_V_DIM) // tn
    sbq_hi = sbq_lo + SB_HEADS * QK_DIM // tn
    s_blocks = seq // tm
    rope_spec = pl.BlockSpec((tm, QK_DIM), lambda i, j: (i % s_blocks, 0))
    kern = functools.partial(_in_proj_kernel, tn=tn, n_qk_tiles=n_qk_tiles,
                             sbq_lo=sbq_lo, sbq_hi=sbq_hi, scale=QK_DIM ** -0.5 * LOG2E)
    return pl.pallas_call(
        kern,
        grid=(t // tm, n // tn),
        in_specs=[
            pl.BlockSpec((tm, d), lambda i, j: (i, 0)),
            pl.BlockSpec((1, d), lambda i, j: (0, 0)),
            pl.BlockSpec((d, tn), lambda i, j: (0, j)),
            pl.BlockSpec((1, QK_DIM), lambda i, j: (0, 0)),
            pl.BlockSpec((1, QK_DIM), lambda i, j: (0, 0)),
            rope_spec, rope_spec, rope_spec,
        ],
        out_specs=pl.BlockSpec((tm, tn), lambda i, j: (i, j)),
        out_shape=jax.ShapeDtypeStruct((t, n), BF16),
        scratch_shapes=[pltpu.VMEM((tm, d), BF16)],
        compiler_params=_params("parallel", "arbitrary"),
        name="in_proj",
    )(x, norm_g, w, qg, kg, cos_t, sa_t, sb_t)


def _diff_attn_kernel(q1_ref, q2_ref, k1_ref, k2_ref, v_ref, lq1_ref, lk1_ref, lq2_ref,
                      lk2_ref, sub_ref, o_ref, vt_ref, sa_ref, sb_ref, m_ref, l_ref, acc_ref, *, blk, heads):
    qi = pl.program_id(2)

    @pl.when(qi == 0)
    def _():
        vt_ref[...] = v_ref[...].astype(F32).T.astype(BF16)

    q_refs = (q1_ref, q2_ref)
    k_refs = (k1_ref, k2_ref)
    krow = lax.broadcasted_iota(jnp.int32, (blk, blk), 0)
    kcol = lax.broadcasted_iota(jnp.int32, (blk, blk), 1)

    chains = [(g, mp) for g in range(heads) for mp in range(2)]

    def scores(j, s_ref):
        start = pl.multiple_of(j * blk, blk)
        for idx, (g, mp) in enumerate(chains):
            hd = slice(g * QK_DIM, (g + 1) * QK_DIM)
            s_ref[idx] = _dot_nt(k_refs[mp][pl.ds(start, blk), hd], q_refs[mp][:, hd])

    def consume(j, s_ref, diag):
        start = pl.multiple_of(j * blk, blk)
        probs, alphas = [], []
        for idx in range(len(chains)):
            s = s_ref[idx]
            if diag:
                s = jnp.where(krow <= kcol, s, -jnp.inf)
            m_old = m_ref[idx]
            m = jnp.maximum(m_old, jnp.max(s, axis=0, keepdims=True))
            alpha = jnp.exp2(m_old - m)
            p = jnp.exp2(s - m)
            l_ref[idx] = alpha * l_ref[idx] + jnp.sum(p, axis=0, keepdims=True)
            m_ref[idx] = m
            alphas.append(alpha)
            probs.append(p.astype(BF16))
        for idx, (g, mp) in enumerate(chains):
            vt = vt_ref[g * DIFF_V_DIM:(g + 1) * DIFF_V_DIM, pl.ds(start, blk)]
            acc_ref[idx] = alphas[idx] * acc_ref[idx] + _dot(vt, probs[idx])

    m_ref[...] = jnp.full(m_ref.shape, -jnp.inf, F32)
    l_ref[...] = jnp.zeros(l_ref.shape, F32)
    acc_ref[...] = jnp.zeros(acc_ref.shape, F32)

    scores(0, sa_ref)
    pairs = lax.shift_right_logical(qi, 1)

    def body(p, carry):
        j = 2 * p
        scores(j + 1, sb_ref)
        consume(j, sa_ref, False)
        scores(j + 2, sa_ref)
        consume(j + 1, sb_ref, False)
        return carry

    lax.fori_loop(0, pairs, body, 0)

    @pl.when(qi == 2 * pairs)
    def _():
        consume(qi, sa_ref, True)

    @pl.when(qi != 2 * pairs)
    def _():
        scores(qi, sb_ref)
        consume(qi - 1, sa_ref, False)
        consume(qi, sb_ref, True)

    lam = (jnp.exp(jnp.sum(lq1_ref[...] * lk1_ref[...], axis=-1, keepdims=True))
           - jnp.exp(jnp.sum(lq2_ref[...] * lk2_ref[...], axis=-1, keepdims=True))
           + LAMBDA_INIT)
    for g in range(heads):
        o = (acc_ref[2 * g] * (1.0 / l_ref[2 * g])
             - acc_ref[2 * g + 1] * (lam / l_ref[2 * g + 1]))
        ms = jnp.mean(o * o, axis=0, keepdims=True)
        o = (o * lax.rsqrt(ms + EPS)).T * sub_ref[...] * (1.0 - LAMBDA_INIT)
        o_ref[:, g * DIFF_V_DIM:(g + 1) * DIFF_V_DIM] = o.astype(o_ref.dtype)


def _diff_attn(u, lq1, lk1, lq2, lk2, subln, *, batch, seq, blk, heads):
    t = u.shape[0]
    nq = seq // blk
    ng = DIFF_HEADS // heads
    assert DIFF_HEADS % heads == 0 and seq % blk == 0
    v_off = 4 * DIFF_HEADS * QK_DIM // (heads * DIFF_V_DIM)
    lam_spec = pl.BlockSpec((1, QK_DIM), lambda b, h, q: (0, 0))
    return pl.pallas_call(
        functools.partial(_diff_attn_kernel, blk=blk, heads=heads),
        grid=(batch, ng, nq),
        in_specs=[
            pl.BlockSpec((blk, heads * QK_DIM), lambda b, h, q: (b * nq + q, h)),
            pl.BlockSpec((blk, heads * QK_DIM), lambda b, h, q: (b * nq + q, ng + h)),
            pl.BlockSpec((seq, heads * QK_DIM), lambda b, h, q: (b, 2 * ng + h)),
            pl.BlockSpec((seq, heads * QK_DIM), lambda b, h, q: (b, 3 * ng + h)),
            pl.BlockSpec((seq, heads * DIFF_V_DIM), lambda b, h, q: (b, v_off + h)),
            lam_spec, lam_spec, lam_spec, lam_spec,
            pl.BlockSpec((1, DIFF_V_DIM), lambda b, h, q: (0, 0)),
        ],
        out_specs=pl.BlockSpec((blk, heads * DIFF_V_DIM), lambda b, h, q: (b * nq + q, h)),
        out_shape=jax.ShapeDtypeStruct((t, DIFF_HEADS * DIFF_V_DIM), BF16),
        scratch_shapes=[
            pltpu.VMEM((heads * DIFF_V_DIM, seq), BF16),
            pltpu.VMEM((2 * heads, blk, blk), F32),
            pltpu.VMEM((2 * heads, blk, blk), F32),
            pltpu.VMEM((2 * heads, 1, blk), F32),
            pltpu.VMEM((2 * heads, 1, blk), F32),
            pltpu.VMEM((2 * heads, DIFF_V_DIM, blk), F32),
        ],
        compiler_params=_params("parallel", "parallel", "arbitrary"),
        name="diff_attn",
    )(u, u, u, u, u, lq1, lk1, lq2, lk2, subln)


def _sb_attn_kernel(q_ref, k_ref, v_ref, o_ref, vt_ref, up_ref, sa_ref, sb_ref, c_ref, acc_ref,
                    *, blk, heads):
    qi = pl.program_id(2)
    krow = lax.broadcasted_iota(jnp.int32, (blk, blk), 0)
    kcol = lax.broadcasted_iota(jnp.int32, (blk, blk), 1)

    @pl.when(qi == 0)
    def _():
        vt_ref[...] = v_ref[...].astype(F32).T.astype(BF16)
        upper = (kcol >= krow).astype(BF16)
        up_ref[...] = jnp.concatenate([upper, upper], axis=1)

    heads_hd = [slice(g * QK_DIM, (g + 1) * QK_DIM) for g in range(heads)]

    def scores(j, s_ref):
        start = pl.multiple_of(j * blk, blk)
        for g, hd in enumerate(heads_hd):
            s_ref[g] = _dot_nt(k_ref[pl.ds(start, blk), hd], q_ref[:, hd])

    def consume(j, s_ref, diag):
        start = pl.multiple_of(j * blk, blk)
        logits, hilo = [], []
        for g in range(heads):
            z = s_ref[g]
            if diag:
                z = jnp.where(krow < kcol, z, -MASKED_LOGIT)
            sp = jnp.maximum(z, 0.0) + jnp.log2(1.0 + jnp.exp2(-jnp.abs(z)))
            if diag:
                c_ref[g] = jnp.sum(sp, axis=0, keepdims=True)
            else:
                shift = c_ref[g]
                c_ref[g] = shift + jnp.sum(sp, axis=0, keepdims=True)
                z = z - shift
            logits.append(z)
            hi = sp.astype(BF16)
            lo = (sp - hi.astype(F32)).astype(BF16)
            hilo.append(jnp.concatenate([hi, lo], axis=0))
        tails = [_dot(up_ref[...], x) for x in hilo]
        weights = [jnp.exp2(logits[g] - tails[g]).astype(BF16) for g in range(heads)]
        for g, hd in enumerate(heads_hd):
            pv = _dot(vt_ref[hd, pl.ds(start, blk)], weights[g])
            if diag:
                acc_ref[g] = pv
            else:
                acc_ref[g] += pv

    def block(i):
        return jnp.maximum(qi - i, 0)

    scores(qi, sa_ref)
    scores(block(1), sb_ref)
    consume(qi, sa_ref, True)
    pairs = lax.shift_right_logical(qi, 1)

    def body(p, carry):
        i = 1 + 2 * p
        scores(block(i + 1), sa_ref)
        consume(block(i), sb_ref, False)
        scores(block(i + 2), sb_ref)
        consume(block(i + 1), sa_ref, False)
        return carry

    lax.fori_loop(0, pairs, body, 0)

    @pl.when(qi != 2 * pairs)
    def _():
        consume(0, sb_ref, False)

    for g in range(heads):
        o_ref[:, g * QK_DIM:(g + 1) * QK_DIM] = acc_ref[g].T.astype(o_ref.dtype)


def _sb_attn(u, *, batch, seq, blk, heads):
    t = u.shape[0]
    nq = seq // blk
    ng = SB_HEADS // heads
    assert SB_HEADS % heads == 0 and seq % blk == 0
    w = heads * QK_DIM
    off = (4 * DIFF_HEADS * QK_DIM + DIFF_HEADS * DIFF_V_DIM) // w
    return pl.pallas_call(
        functools.partial(_sb_attn_kernel, blk=blk, heads=heads),
        grid=(batch, ng, nq),
        in_specs=[
            pl.BlockSpec((blk, w), lambda b, h, q: (b * nq + q, off + h)),
            pl.BlockSpec((seq, w), lambda b, h, q: (b, off + ng + h)),
            pl.BlockSpec((seq, w), lambda b, h, q: (b, off + 2 * ng + h)),
        ],
        out_specs=pl.BlockSpec((blk, w), lambda b, h, q: (b * nq + q, h)),
        out_shape=jax.ShapeDtypeStruct((t, SB_HEADS * QK_DIM), BF16),
        scratch_shapes=[
            pltpu.VMEM((w, seq), BF16),
            pltpu.VMEM((blk, 2 * blk), BF16),
            pltpu.VMEM((heads, blk, blk), F32),
            pltpu.VMEM((heads, blk, blk), F32),
            pltpu.VMEM((heads, 1, blk), F32),
            pltpu.VMEM((heads, QK_DIM, blk), F32),
        ],
        compiler_params=_params("parallel", "parallel", "arbitrary"),
        name="sb_attn",
    )(u, u, u)


def _out_proj_kernel(x_ref, a_ref, b_ref, wa_ref, wb_ref, o_ref):
    o_ref[...] = x_ref[...] + _dot(a_ref[...], wa_ref[...]) + _dot(b_ref[...], wb_ref[...])


def _out_proj(x, a, b, w, *, tm):
    t, d = x.shape
    wa_rows, wb_rows = a.shape[1], b.shape[1]
    assert t % tm == 0 and wa_rows == wb_rows and w.shape[0] == wa_rows + wb_rows
    return pl.pallas_call(
        _out_proj_kernel,
        grid=(t // tm,),
        in_specs=[
            pl.BlockSpec((tm, d), lambda i: (i, 0)),
            pl.BlockSpec((tm, wa_rows), lambda i: (i, 0)),
            pl.BlockSpec((tm, wb_rows), lambda i: (i, 0)),
            pl.BlockSpec((wa_rows, d), lambda i: (0, 0)),
            pl.BlockSpec((wb_rows, d), lambda i: (1, 0)),
        ],
        out_specs=pl.BlockSpec((tm, d), lambda i: (i, 0)),
        out_shape=jax.ShapeDtypeStruct((t, d), F32),
        compiler_params=_params("parallel"),
        name="out_proj",
    )(x, a, b, w, w)


def _rope_tables(seq):
    half = ROPE_DIM // 2
    pos = jnp.arange(seq, dtype=F32)
    inv_freq = ROPE_THETA ** (-jnp.arange(0, ROPE_DIM, 2, dtype=F32) / ROPE_DIM)
    ang = pos[:, None] * inv_freq[None, :]
    cos, sin = jnp.cos(ang), jnp.sin(ang)
    ones = jnp.ones((seq, QK_DIM - ROPE_DIM), F32)
    zeros_h = jnp.zeros((seq, half), F32)
    zeros_r = jnp.zeros((seq, QK_DIM - ROPE_DIM), F32)
    cos_t = jnp.concatenate([cos, cos, ones], axis=1)
    sa_t = jnp.concatenate([-sin, zeros_h, zeros_r], axis=1)
    sb_t = jnp.concatenate([zeros_h, sin, zeros_r], axis=1)
    return cos_t, sa_t, sb_t


def _pick(n, pref):
    for c in pref:
        if n % c == 0:
            return c
    return n


def kernel(x, ffn1_norm, ffn1_w_gate, ffn1_w_up, ffn1_w_down, mix_norm, w_in, q_norm, k_norm,
           lambda_q1, lambda_k1, lambda_q2, lambda_k2, subln, w_out,
           ffn2_norm, ffn2_w_gate, ffn2_w_up, ffn2_w_down, final_norm):
    batch, seq, d = x.shape
    assert ffn1_norm.shape[0] == 1, "single-layer block"
    t = batch * seq
    l = 0
    dff = ffn1_w_gate.shape[-1]
    tm = _pick(t, (1024, 512, 256, 128))
    tf = _pick(dff, (256, 128))
    blk = _pick(seq, (256, 128))
    bf = lambda w: w[l].astype(BF16)
    mat = lambda w: w.reshape(w.shape[1:])
    row = lambda g: g[l].astype(F32)[None, :]

    xt = x.reshape(t, d)
    x1 = _ffn(xt, row(ffn1_norm), mat(ffn1_w_gate), mat(ffn1_w_up), mat(ffn1_w_down),
              row(final_norm), final_norm=False, tm=tm, tf=tf)

    cos_t, sa_t, sb_t = _rope_tables(seq)
    u = _in_proj(x1, row(mix_norm), mat(w_in), row(q_norm), row(k_norm), cos_t, sa_t, sb_t,
                 seq=seq, tm=_pick(seq, (1024, 512, 256, 128)), tn=512)

    a = _diff_attn(u, row(lambda_q1), row(lambda_k1), row(lambda_q2), row(lambda_k2),
                   row(subln), batch=batch, seq=seq, blk=blk, heads=2)
    b = _sb_attn(u, batch=batch, seq=seq, blk=blk, heads=4)

    x2 = _out_proj(x1, a, b, bf(w_out), tm=_pick(t, (512, 256, 128)))

    out = _ffn(x2, row(ffn2_norm), mat(ffn2_w_gate), mat(ffn2_w_up), mat(ffn2_w_down),
               row(final_norm), final_norm=True, tm=tm, tf=tf)
    return out.reshape(batch, seq, d)
```

```python
import functools
import math

import jax
import jax.numpy as jnp
from jax import lax
from jax.experimental import pallas as pl
from jax.experimental.pallas import tpu as pltpu

EPS = 1e-5
ROPE_THETA = 500000.0
LAMBDA_INIT = 0.8 - 0.6 * math.exp(-0.3 * 0)
LOG2E = math.log2(math.e)
MASKED_LOGIT = 1e30

DIFF_HEADS = 4
SB_HEADS = 8
QK_DIM = 128
DIFF_V_DIM = 2 * QK_DIM
ROPE_DIM = QK_DIM // 4

F32 = jnp.float32
BF16 = jnp.bfloat16

VMEM_LIMIT_BYTES = 56 * 1024 * 1024
ROW_CHUNK = 256


def _params(*sem):
    return pltpu.CompilerParams(dimension_semantics=sem, vmem_limit_bytes=VMEM_LIMIT_BYTES)


def _rms(x, g):
    ms = jnp.mean(x * x, axis=-1, keepdims=True)
    return x * lax.rsqrt(ms + EPS) * g


def _dot(a, b):
    return jnp.dot(a, b, preferred_element_type=F32)


def _dot_nt(a, b):
    return lax.dot_general(a, b, (((1,), (1,)), ((), ())), preferred_element_type=F32)


def _ffn_kernel(x_ref, g_ref, wg_ref, wu_ref, wd_ref, fg_ref, o_ref, xn_ref, *, final_norm):
    f = pl.program_id(1)

    @pl.when(f == 0)
    def _():
        x = x_ref[...]
        xn_ref[...] = _rms(x, g_ref[...]).astype(BF16)
        o_ref[...] = x

    xn = xn_ref[...]
    gate = _dot(xn, wg_ref[...].astype(BF16))
    up = _dot(xn, wu_ref[...].astype(BF16))
    h = (gate * jax.nn.sigmoid(gate)) * (up * 0.5)
    o_ref[...] += _dot(h.astype(BF16), wd_ref[...].astype(BF16))

    if final_norm:
        @pl.when(f == pl.num_programs(1) - 1)
        def _():
            o_ref[...] = _rms(o_ref[...], fg_ref[...])


def _ffn(x, norm_g, wg, wu, wd, final_g, *, final_norm, tm, tf):
    t, d = x.shape
    dff = wg.shape[1]
    assert t % tm == 0 and dff % tf == 0
    return pl.pallas_call(
        functools.partial(_ffn_kernel, final_norm=final_norm),
        grid=(t // tm, dff // tf),
        in_specs=[
            pl.BlockSpec((tm, d), lambda i, f: (i, 0)),
            pl.BlockSpec((1, d), lambda i, f: (0, 0)),
            pl.BlockSpec((d, tf), lambda i, f: (0, f)),
            pl.BlockSpec((d, tf), lambda i, f: (0, f)),
            pl.BlockSpec((tf, d), lambda i, f: (f, 0)),
            pl.BlockSpec((1, d), lambda i, f: (0, 0)),
        ],
        out_specs=pl.BlockSpec((tm, d), lambda i, f: (i, 0)),
        out_shape=jax.ShapeDtypeStruct((t, d), F32),
        scratch_shapes=[pltpu.VMEM((tm, d), BF16)],
        compiler_params=_params("parallel", "arbitrary"),
        name="ffn_final" if final_norm else "ffn",
    )(x, norm_g, wg, wu, wd, final_g)


def _in_proj_kernel(x_ref, g_ref, w_ref, qg_ref, kg_ref, cos_ref, sa_ref, sb_ref,
                    o_ref, xn_ref, *, tn, n_qk_tiles, sbq_lo, sbq_hi, scale):
    j = pl.program_id(1)

    @pl.when(j == 0)
    def _():
        xn_ref[...] = _rms(x_ref[...], g_ref[...]).astype(BF16)

    tm = o_ref.shape[0]
    chunk = min(tm, ROW_CHUNK)

    @pl.when(j < n_qk_tiles)
    def _():
        w = w_ref[:, pl.ds(pl.multiple_of(j * tn, tn), tn)]
        is_q = j < n_qk_tiles // 2
        gain = jnp.where(is_q, qg_ref[...], kg_ref[...])
        post = jnp.where(is_q, scale, 1.0).astype(F32)
        for r in range(tm // chunk):
            rows = slice(r * chunk, (r + 1) * chunk)
            u = _dot(xn_ref[rows, :], w)
            cos, sa, sb = cos_ref[rows, :], sa_ref[rows, :], sb_ref[rows, :]
            for c in range(tn // QK_DIM):
                t = u[:, c * QK_DIM:(c + 1) * QK_DIM]
                ms = jnp.mean(t * t, axis=-1, keepdims=True)
                t = t * (lax.rsqrt(ms + EPS) * post) * gain
                t = (t * cos + pltpu.roll(t, QK_DIM - ROPE_DIM // 2, 1) * sa
                     + pltpu.roll(t, ROPE_DIM // 2, 1) * sb)
                o_ref[rows, c * QK_DIM:(c + 1) * QK_DIM] = t.astype(BF16)

    @pl.when(j >= n_qk_tiles)
    def _():
        post = jnp.where((j >= sbq_lo) & (j < sbq_hi), scale, 1.0).astype(F32)
        u = _dot(xn_ref[...], w_ref[:, pl.ds(pl.multiple_of(j * tn, tn), tn)])
        o_ref[...] = (u * post).astype(BF16)


def _in_proj(x, norm_g, w, qg, kg, cos_t, sa_t, sb_t, *, seq, tm, tn):
    t, d = x.shape
    n = w.shape[1]
    diff_qk_cols = DIFF_HEADS * QK_DIM
    assert t % tm == 0 and seq % tm == 0 and n % tn == 0 and diff_qk_cols % tn == 0
    n_qk_tiles = 4 * diff_qk_cols // tn
    sbq_lo = (4 * diff_qk_cols + DIFF_HEADS * DIFF_V_DIM) // tn
    sbq_hi = sbq_lo + SB_HEADS * QK_DIM // tn
    s_blocks = seq // tm
    rope_spec = pl.BlockSpec((tm, QK_DIM), lambda i, j: (i % s_blocks, 0))
    kern = functools.partial(_in_proj_kernel, tn=tn, n_qk_tiles=n_qk_tiles,
                             sbq_lo=sbq_lo, sbq_hi=sbq_hi, scale=QK_DIM ** -0.5 * LOG2E)
    return pl.pallas_call(
        kern,
        grid=(t // tm, n // tn),
        in_specs=[
            pl.BlockSpec((tm, d), lambda i, j: (i, 0)),
            pl.BlockSpec((1, d), lambda i, j: (0, 0)),
            pl.BlockSpec((d, n), lambda i, j: (0, 0), pipeline_mode=pl.Buffered(1)),
            pl.BlockSpec((1, QK_DIM), lambda i, j: (0, 0)),
            pl.BlockSpec((1, QK_DIM), lambda i, j: (0, 0)),
            rope_spec, rope_spec, rope_spec,
        ],
        out_specs=pl.BlockSpec((tm, tn), lambda i, j: (i, j)),
        out_shape=jax.ShapeDtypeStruct((t, n), BF16),
        scratch_shapes=[pltpu.VMEM((tm, d), BF16)],
        compiler_params=_params("parallel", "arbitrary"),
        name="in_proj",
    )(x, norm_g, w, qg, kg, cos_t, sa_t, sb_t)


def _diff_attn_kernel(q1_ref, q2_ref, k1_ref, k2_ref, v_ref, lq1_ref, lk1_ref, lq2_ref,
                      lk2_ref, sub_ref, o_ref, vt_ref, sa_ref, sb_ref, m_ref, l_ref, acc_ref,
                      *, tk, heads):
    qi = pl.program_id(2)
    tq = 2 * tk

    @pl.when(qi == 0)
    def _():
        vt_ref[...] = v_ref[...].astype(F32).T.astype(BF16)

    q_refs = (q1_ref, q2_ref)
    k_refs = (k1_ref, k2_ref)
    krow = lax.broadcasted_iota(jnp.int32, (tk, tq), 0)
    kcol = lax.broadcasted_iota(jnp.int32, (tk, tq), 1)

    chains = [(g, mp) for g in range(heads) for mp in range(2)]

    def scores(j, s_ref):
        start = pl.multiple_of(j * tk, tk)
        for idx, (g, mp) in enumerate(chains):
            hd = slice(g * QK_DIM, (g + 1) * QK_DIM)
            s_ref[idx] = _dot_nt(k_refs[mp][pl.ds(start, tk), hd], q_refs[mp][:, hd])

    def consume(j, s_ref, visible=None):
        start = pl.multiple_of(j * tk, tk)
        probs, alphas = [], []
        for idx in range(len(chains)):
            s = s_ref[idx]
            if visible is not None:
                s = jnp.where(visible, s, -jnp.inf)
            m_old = m_ref[idx]
            m = jnp.maximum(m_old, jnp.max(s, axis=0, keepdims=True))
            alpha = jnp.exp2(m_old - m)
            p = jnp.exp2(s - m)
            l_ref[idx] = alpha * l_ref[idx] + jnp.sum(p, axis=0, keepdims=True)
            m_ref[idx] = m
            alphas.append(alpha)
            probs.append(p.astype(BF16))
        for idx, (g, mp) in enumerate(chains):
            vt = vt_ref[g * DIFF_V_DIM:(g + 1) * DIFF_V_DIM, pl.ds(start, tk)]
            acc_ref[idx] = alphas[idx] * acc_ref[idx] + _dot(vt, probs[idx])

    m_ref[...] = jnp.full(m_ref.shape, -jnp.inf, F32)
    l_ref[...] = jnp.zeros(l_ref.shape, F32)
    acc_ref[...] = jnp.zeros(acc_ref.shape, F32)

    scores(0, sa_ref)

    def body(p, carry):
        j = 2 * p
        scores(j + 1, sb_ref)
        consume(j, sa_ref)
        scores(j + 2, sa_ref)
        consume(j + 1, sb_ref)
        return carry

    lax.fori_loop(0, qi, body, 0)
    scores(2 * qi + 1, sb_ref)
    consume(2 * qi, sa_ref, krow <= kcol)
    consume(2 * qi + 1, sb_ref, krow + tk <= kcol)

    lam = (jnp.exp(jnp.sum(lq1_ref[...] * lk1_ref[...], axis=-1, keepdims=True))
           - jnp.exp(jnp.sum(lq2_ref[...] * lk2_ref[...], axis=-1, keepdims=True))
           + LAMBDA_INIT)
    for g in range(heads):
        o = (acc_ref[2 * g] * (1.0 / l_ref[2 * g])
             - acc_ref[2 * g + 1] * (lam / l_ref[2 * g + 1]))
        ms = jnp.mean(o * o, axis=0, keepdims=True)
        o = (o * lax.rsqrt(ms + EPS)).T * sub_ref[...] * (1.0 - LAMBDA_INIT)
        o_ref[:, g * DIFF_V_DIM:(g + 1) * DIFF_V_DIM] = o.astype(o_ref.dtype)


def _diff_attn(u, lq1, lk1, lq2, lk2, subln, *, batch, seq, tk, heads):
    t = u.shape[0]
    blk = 2 * tk
    nq = seq // blk
    ng = DIFF_HEADS // heads
    assert DIFF_HEADS % heads == 0 and seq % blk == 0
    v_off = 4 * DIFF_HEADS * QK_DIM // (heads * DIFF_V_DIM)
    lam_spec = pl.BlockSpec((1, QK_DIM), lambda b, h, q: (0, 0))
    return pl.pallas_call(
        functools.partial(_diff_attn_kernel, tk=tk, heads=heads),
        grid=(batch, ng, nq),
        in_specs=[
            pl.BlockSpec((blk, heads * QK_DIM), lambda b, h, q: (b * nq + q, h)),
            pl.BlockSpec((blk, heads * QK_DIM), lambda b, h, q: (b * nq + q, ng + h)),
            pl.BlockSpec((seq, heads * QK_DIM), lambda b, h, q: (b, 2 * ng + h)),
            pl.BlockSpec((seq, heads * QK_DIM), lambda b, h, q: (b, 3 * ng + h)),
            pl.BlockSpec((seq, heads * DIFF_V_DIM), lambda b, h, q: (b, v_off + h)),
            lam_spec, lam_spec, lam_spec, lam_spec,
            pl.BlockSpec((1, DIFF_V_DIM), lambda b, h, q: (0, 0)),
        ],
        out_specs=pl.BlockSpec((blk, heads * DIFF_V_DIM), lambda b, h, q: (b * nq + q, h)),
        out_shape=jax.ShapeDtypeStruct((t, DIFF_HEADS * DIFF_V_DIM), BF16),
        scratch_shapes=[
            pltpu.VMEM((heads * DIFF_V_DIM, seq), BF16),
            pltpu.VMEM((2 * heads, tk, blk), F32),
            pltpu.VMEM((2 * heads, tk, blk), F32),
            pltpu.VMEM((2 * heads, 1, blk), F32),
            pltpu.VMEM((2 * heads, 1, blk), F32),
            pltpu.VMEM((2 * heads, DIFF_V_DIM, blk), F32),
        ],
        compiler_params=_params("parallel", "parallel", "arbitrary"),
        name="diff_attn",
    )(u, u, u, u, u, lq1, lk1, lq2, lk2, subln)


def _sb_attn_kernel(q_ref, k_ref, v_ref, o_ref, vt_ref, up_ref, sa_ref, sb_ref, c_ref, acc_ref,
                    *, tk, heads):
    qi = pl.program_id(2)
    tq = 2 * tk
    krow = lax.broadcasted_iota(jnp.int32, (tk, tq), 0)
    kcol = lax.broadcasted_iota(jnp.int32, (tk, tq), 1)

    @pl.when(qi == 0)
    def _():
        vt_ref[...] = v_ref[...].astype(F32).T.astype(BF16)
        r = lax.broadcasted_iota(jnp.int32, (tk, tk), 0)
        c = lax.broadcasted_iota(jnp.int32, (tk, tk), 1)
        upper = (c >= r).astype(BF16)
        up_ref[...] = jnp.concatenate([upper, upper], axis=1)

    heads_hd = [slice(g * QK_DIM, (g + 1) * QK_DIM) for g in range(heads)]

    def scores(j, s_ref):
        start = pl.multiple_of(j * tk, tk)
        for g, hd in enumerate(heads_hd):
            s_ref[g] = _dot_nt(k_ref[pl.ds(start, tk), hd], q_ref[:, hd])

    def consume(j, s_ref, visible=None):
        start = pl.multiple_of(j * tk, tk)
        logits, hilo = [], []
        for g in range(heads):
            z = s_ref[g]
            if visible is not None:
                z = jnp.where(visible, z, -MASKED_LOGIT)
            sp = jnp.maximum(z, 0.0) + jnp.log2(1.0 + jnp.exp2(-jnp.abs(z)))
            shift = c_ref[g]
            c_ref[g] = shift + jnp.sum(sp, axis=0, keepdims=True)
            logits.append(z - shift)
            hi = sp.astype(BF16)
            lo = (sp - hi.astype(F32)).astype(BF16)
            hilo.append(jnp.concatenate([hi, lo], axis=0))
        tails = [_dot(up_ref[...], x) for x in hilo]
        weights = [jnp.exp2(logits[g] - tails[g]).astype(BF16) for g in range(heads)]
        for g, hd in enumerate(heads_hd):
            acc_ref[g] += _dot(vt_ref[hd, pl.ds(start, tk)], weights[g])

    c_ref[...] = jnp.zeros(c_ref.shape, F32)
    acc_ref[...] = jnp.zeros(acc_ref.shape, F32)

    last = 2 * qi + 1

    def block(i):
        return jnp.maximum(last - i, 0)

    scores(last, sa_ref)
    scores(block(1), sb_ref)
    consume(last, sa_ref, krow + tk < kcol)
    scores(block(2), sa_ref)
    consume(block(1), sb_ref, krow < kcol)

    def body(p, carry):
        i = 2 + 2 * p
        scores(block(i + 1), sb_ref)
        consume(block(i), sa_ref)
        scores(block(i + 2), sa_ref)
        consume(block(i + 1), sb_ref)
        return carry

    lax.fori_loop(0, qi, body, 0)

    for g in range(heads):
        o_ref[:, g * QK_DIM:(g + 1) * QK_DIM] = acc_ref[g].T.astype(o_ref.dtype)


def _sb_attn(u, *, batch, seq, tk, heads):
    t = u.shape[0]
    blk = 2 * tk
    nq = seq // blk
    ng = SB_HEADS // heads
    assert SB_HEADS % heads == 0 and seq % blk == 0
    w = heads * QK_DIM
    off = (4 * DIFF_HEADS * QK_DIM + DIFF_HEADS * DIFF_V_DIM) // w
    return pl.pallas_call(
        functools.partial(_sb_attn_kernel, tk=tk, heads=heads),
        grid=(batch, ng, nq),
        in_specs=[
            pl.BlockSpec((blk, w), lambda b, h, q: (b * nq + q, off + h)),
            pl.BlockSpec((seq, w), lambda b, h, q: (b, off + ng + h)),
            pl.BlockSpec((seq, w), lambda b, h, q: (b, off + 2 * ng + h)),
        ],
        out_specs=pl.BlockSpec((blk, w), lambda b, h, q: (b * nq + q, h)),
        out_shape=jax.ShapeDtypeStruct((t, SB_HEADS * QK_DIM), BF16),
        scratch_shapes=[
            pltpu.VMEM((w, seq), BF16),
            pltpu.VMEM((tk, 2 * tk), BF16),
            pltpu.VMEM((heads, tk, blk), F32),
            pltpu.VMEM((heads, tk, blk), F32),
            pltpu.VMEM((heads, 1, blk), F32),
            pltpu.VMEM((heads, QK_DIM, blk), F32),
        ],
        compiler_params=_params("parallel", "parallel", "arbitrary"),
        name="sb_attn",
    )(u, u, u)


def _out_proj_kernel(x_ref, a_ref, b_ref, wa_ref, wb_ref, o_ref):
    o_ref[...] = x_ref[...] + _dot(a_ref[...], wa_ref[...]) + _dot(b_ref[...], wb_ref[...])


def _out_proj(x, a, b, w, *, tm):
    t, d = x.shape
    wa_rows, wb_rows = a.shape[1], b.shape[1]
    assert t % tm == 0 and wa_rows == wb_rows and w.shape[0] == wa_rows + wb_rows
    return pl.pallas_call(
        _out_proj_kernel,
        grid=(t // tm,),
        in_specs=[
            pl.BlockSpec((tm, d), lambda i: (i, 0)),
            pl.BlockSpec((tm, wa_rows), lambda i: (i, 0)),
            pl.BlockSpec((tm, wb_rows), lambda i: (i, 0)),
            pl.BlockSpec((wa_rows, d), lambda i: (0, 0)),
            pl.BlockSpec((wb_rows, d), lambda i: (1, 0)),
        ],
        out_specs=pl.BlockSpec((tm, d), lambda i: (i, 0)),
        out_shape=jax.ShapeDtypeStruct((t, d), F32),
        compiler_params=_params("parallel"),
        name="out_proj",
    )(x, a, b, w, w)


def _rope_tables(seq):
    half = ROPE_DIM // 2
    pos = jnp.arange(seq, dtype=F32)
    inv_freq = ROPE_THETA ** (-jnp.arange(0, ROPE_DIM, 2, dtype=F32) / ROPE_DIM)
    ang = pos[:, None] * inv_freq[None, :]
    cos, sin = jnp.cos(ang), jnp.sin(ang)
    ones = jnp.ones((seq, QK_DIM - ROPE_DIM), F32)
    zeros_h = jnp.zeros((seq, half), F32)
    zeros_r = jnp.zeros((seq, QK_DIM - ROPE_DIM), F32)
    cos_t = jnp.concatenate([cos, cos, ones], axis=1)
    sa_t = jnp.concatenate([-sin, zeros_h, zeros_r], axis=1)
    sb_t = jnp.concatenate([zeros_h, sin, zeros_r], axis=1)
    return cos_t, sa_t, sb_t


def _pick(n, pref):
    for c in pref:
        if n % c == 0:
            return c
    return n


def kernel(x, ffn1_norm, ffn1_w_gate, ffn1_w_up, ffn1_w_down, mix_norm, w_in, q_norm, k_norm,
           lambda_q1, lambda_k1, lambda_q2, lambda_k2, subln, w_out,
           ffn2_norm, ffn2_w_gate, ffn2_w_up, ffn2_w_down, final_norm):
    batch, seq, d = x.shape
    assert ffn1_norm.shape[0] == 1, "single-layer block"
    t = batch * seq
    l = 0
    dff = ffn1_w_gate.shape[-1]
    tm = _pick(t, (1024, 512, 256, 128))
    tf = _pick(dff, (256, 128))
    tk = _pick(seq // 2, (256, 128))
    bf = lambda w: w[l].astype(BF16)
    mat = lambda w: w.reshape(w.shape[1:])
    row = lambda g: g[l].astype(F32)[None, :]

    xt = x.reshape(t, d)
    x1 = _ffn(xt, row(ffn1_norm), mat(ffn1_w_gate), mat(ffn1_w_up), mat(ffn1_w_down),
              row(final_norm), final_norm=False, tm=tm, tf=tf)

    cos_t, sa_t, sb_t = _rope_tables(seq)
    u = _in_proj(x1, row(mix_norm), bf(w_in), row(q_norm), row(k_norm), cos_t, sa_t, sb_t,
                 seq=seq, tm=_pick(seq, (1024, 512, 256, 128)), tn=512)

    a = _diff_attn(u, row(lambda_q1), row(lambda_k1), row(lambda_q2), row(lambda_k2),
                   row(subln), batch=batch, seq=seq, tk=tk, heads=2)
    b = _sb_attn(u, batch=batch, seq=seq, tk=tk, heads=4)

    x2 = _out_proj(x1, a, b, bf(w_out), tm=_pick(t, (512, 256, 128)))

    out = _ffn(x2, row(ffn2_norm), mat(ffn2_w_gate), mat(ffn2_w_up), mat(ffn2_w_down),
               row(final_norm), final_norm=True, tm=tm, tf=tf)
    return out.reshape(batch, seq, d)
```

```python
import functools
import math

import jax
import jax.numpy as jnp
from jax import lax
from jax.experimental import pallas as pl
from jax.experimental.pallas import tpu as pltpu

EPS = 1e-5
ROPE_THETA = 500000.0
LAMBDA_INIT = 0.8 - 0.6 * math.exp(-0.3 * 0)
LOG2E = math.log2(math.e)
MASKED_LOGIT = 1e30

DIFF_HEADS = 4
SB_HEADS = 8
QK_DIM = 128
DIFF_V_DIM = 2 * QK_DIM
ROPE_DIM = QK_DIM // 4

F32 = jnp.float32
BF16 = jnp.bfloat16

VMEM_LIMIT_BYTES = 56 * 1024 * 1024
ROW_CHUNK = 256
SUM_ROWS = 16
C_ROWS = 8


def _params(*sem):
    return pltpu.CompilerParams(dimension_semantics=sem, vmem_limit_bytes=VMEM_LIMIT_BYTES)


def _rms(x, g):
    ms = jnp.mean(x * x, axis=-1, keepdims=True)
    return x * lax.rsqrt(ms + EPS) * g


def _dot(a, b):
    return jnp.dot(a, b, preferred_element_type=F32)


def _dot_nt(a, b):
    return lax.dot_general(a, b, (((1,), (1,)), ((), ())), preferred_element_type=F32)


def _ffn_kernel(x_ref, g_ref, wg_ref, wu_ref, wd_ref, fg_ref, o_ref, xn_ref, *, final_norm):
    f = pl.program_id(1)
    last_f = pl.num_programs(1) - 1
    tm = o_ref.shape[0]

    def step(chunk, first, last):
        wg, wu, wd = (w[...].astype(BF16) for w in (wg_ref, wu_ref, wd_ref))
        for r in range(tm // chunk):
            rows = slice(r * chunk, (r + 1) * chunk)
            if first:
                base = x_ref[rows, :]
                xn = _rms(base, g_ref[...]).astype(BF16)
                xn_ref[rows, :] = xn
            else:
                base = o_ref[rows, :]
                xn = xn_ref[rows, :]
            gate = _dot(xn, wg)
            up = _dot(xn, wu)
            h = (gate * jax.nn.sigmoid(gate)) * (up * 0.5)
            o = base + _dot(h.astype(BF16), wd)
            o_ref[rows, :] = _rms(o, fg_ref[...]) if last else o

    small = min(tm, ROW_CHUNK)

    @pl.when(f == 0)
    def _():
        step(small, True, False)

    if final_norm:
        @pl.when((f > 0) & (f < last_f))
        def _():
            step(tm, False, False)

        @pl.when(f == last_f)
        def _():
            step(small, False, True)
    else:
        @pl.when(f > 0)
        def _():
            step(tm, False, False)


def _ffn(x, norm_g, wg, wu, wd, final_g, *, final_norm, tm, tf):
    t, d = x.shape
    dff = wg.shape[1]
    assert t % tm == 0 and dff % tf == 0 and dff // tf >= 2
    return pl.pallas_call(
        functools.partial(_ffn_kernel, final_norm=final_norm),
        grid=(t // tm, dff // tf),
        in_specs=[
            pl.BlockSpec((tm, d), lambda i, f: (i, 0)),
            pl.BlockSpec((1, d), lambda i, f: (0, 0)),
            pl.BlockSpec((d, tf), lambda i, f: (0, f)),
            pl.BlockSpec((d, tf), lambda i, f: (0, f)),
            pl.BlockSpec((tf, d), lambda i, f: (f, 0)),
            pl.BlockSpec((1, d), lambda i, f: (0, 0)),
        ],
        out_specs=pl.BlockSpec((tm, d), lambda i, f: (i, 0)),
        out_shape=jax.ShapeDtypeStruct((t, d), F32),
        scratch_shapes=[pltpu.VMEM((tm, d), BF16)],
        compiler_params=_params("parallel", "arbitrary"),
        name="ffn_final" if final_norm else "ffn",
    )(x, norm_g, wg, wu, wd, final_g)


def _in_proj_kernel(x_ref, g_ref, w_ref, qg_ref, kg_ref, cos_ref, sa_ref, sb_ref,
                    o_ref, xn_ref, *, tn, n_qk_tiles, sbq_lo, sbq_hi, scale):
    j = pl.program_id(1)

    tm = o_ref.shape[0]
    chunk = min(tm, ROW_CHUNK)

    def qk_tile(first):
        w = w_ref[:, pl.ds(pl.multiple_of(j * tn, tn), tn)]
        is_q = j < n_qk_tiles // 2
        gain = jnp.where(is_q, qg_ref[...], kg_ref[...])
        post = jnp.where(is_q, scale, 1.0).astype(F32)
        for r in range(tm // chunk):
            rows = slice(r * chunk, (r + 1) * chunk)
            if first:
                xn = _rms(x_ref[rows, :], g_ref[...]).astype(BF16)
                xn_ref[rows, :] = xn
            else:
                xn = xn_ref[rows, :]
            u = _dot(xn, w)
            cos, sa, sb = cos_ref[rows, :], sa_ref[rows, :], sb_ref[rows, :]
            for c in range(tn // QK_DIM):
                t = u[:, c * QK_DIM:(c + 1) * QK_DIM]
                ms = jnp.mean(t * t, axis=-1, keepdims=True)
                t = t * (lax.rsqrt(ms + EPS) * post) * gain
                t = (t * cos + pltpu.roll(t, QK_DIM - ROPE_DIM // 2, 1) * sa
                     + pltpu.roll(t, ROPE_DIM // 2, 1) * sb)
                o_ref[rows, c * QK_DIM:(c + 1) * QK_DIM] = t.astype(BF16)

    @pl.when(j == 0)
    def _():
        qk_tile(True)

    @pl.when((j > 0) & (j < n_qk_tiles))
    def _():
        qk_tile(False)

    @pl.when(j >= n_qk_tiles)
    def _():
        post = jnp.where((j >= sbq_lo) & (j < sbq_hi), scale, 1.0).astype(F32)
        u = _dot(xn_ref[...], w_ref[:, pl.ds(pl.multiple_of(j * tn, tn), tn)])
        o_ref[...] = (u * post).astype(BF16)


def _in_proj(x, norm_g, w, qg, kg, cos_t, sa_t, sb_t, *, seq, tm, tn):
    t, d = x.shape
    n = w.shape[1]
    diff_qk_cols = DIFF_HEADS * QK_DIM
    assert t % tm == 0 and seq % tm == 0 and n % tn == 0 and diff_qk_cols % tn == 0
    n_qk_tiles = 4 * diff_qk_cols // tn
    sbq_lo = (4 * diff_qk_cols + DIFF_HEADS * DIFF_V_DIM) // tn
    sbq_hi = sbq_lo + SB_HEADS * QK_DIM // tn
    s_blocks = seq // tm
    rope_spec = pl.BlockSpec((tm, QK_DIM), lambda i, j: (i % s_blocks, 0))
    kern = functools.partial(_in_proj_kernel, tn=tn, n_qk_tiles=n_qk_tiles,
                             sbq_lo=sbq_lo, sbq_hi=sbq_hi, scale=QK_DIM ** -0.5 * LOG2E)
    return pl.pallas_call(
        kern,
        grid=(t // tm, n // tn),
        in_specs=[
            pl.BlockSpec((tm, d), lambda i, j: (i, 0)),
            pl.BlockSpec((1, d), lambda i, j: (0, 0)),
            pl.BlockSpec((d, n), lambda i, j: (0, 0), pipeline_mode=pl.Buffered(1)),
            pl.BlockSpec((1, QK_DIM), lambda i, j: (0, 0)),
            pl.BlockSpec((1, QK_DIM), lambda i, j: (0, 0)),
            rope_spec, rope_spec, rope_spec,
        ],
        out_specs=pl.BlockSpec((tm, tn), lambda i, j: (i, j)),
        out_shape=jax.ShapeDtypeStruct((t, n), BF16),
        scratch_shapes=[pltpu.VMEM((tm, d), BF16)],
        compiler_params=_params("parallel", "arbitrary"),
        name="in_proj",
    )(x, norm_g, w, qg, kg, cos_t, sa_t, sb_t)


def _diff_attn_kernel(q1_ref, q2_ref, k1_ref, k2_ref, v_ref, lq1_ref, lk1_ref, lq2_ref,
                      lk2_ref, sub_ref, o_ref, vt_ref, sa_ref, sb_ref, m_ref, l_ref, acc_ref,
                      *, tk, heads):
    qi = pl.program_id(2)
    tq = 2 * tk

    @pl.when(qi == 0)
    def _():
        vt_ref[...] = v_ref[...].astype(F32).T.astype(BF16)

    q_refs = (q1_ref, q2_ref)
    k_refs = (k1_ref, k2_ref)
    krow = lax.broadcasted_iota(jnp.int32, (tk, tq), 0)
    kcol = lax.broadcasted_iota(jnp.int32, (tk, tq), 1)

    chains = [(g, mp) for g in range(heads) for mp in range(2)]

    every_query = slice(0, tq)
    late_queries = slice(tk, tq)

    def scores(j, s_ref, qs=every_query):
        start = pl.multiple_of(j * tk, tk)
        width = qs.stop - qs.start
        for idx, (g, mp) in enumerate(chains):
            hd = slice(g * QK_DIM, (g + 1) * QK_DIM)
            s_ref[idx, :, :width] = _dot_nt(k_refs[mp][pl.ds(start, tk), hd], q_refs[mp][qs, hd])

    def consume(j, s_ref, visible=None, qs=every_query):
        start = pl.multiple_of(j * tk, tk)
        width = qs.stop - qs.start
        probs, alphas = [], []
        for idx in range(len(chains)):
            s = s_ref[idx, :, :width]
            if visible is not None:
                s = jnp.where(visible, s, -jnp.inf)
            m_old = m_ref[idx, :, qs]
            m = jnp.maximum(m_old, jnp.max(s, axis=0, keepdims=True))
            alpha = jnp.exp2(m_old - m)
            p = jnp.exp2(s - m)
            l_ref[idx, :, qs] = alpha * l_ref[idx, :, qs] + jnp.sum(p, axis=0, keepdims=True)
            m_ref[idx, :, qs] = m
            alphas.append(alpha)
            probs.append(p.astype(BF16))
        for idx, (g, mp) in enumerate(chains):
            vt = vt_ref[g * DIFF_V_DIM:(g + 1) * DIFF_V_DIM, pl.ds(start, tk)]
            acc_ref[idx, :, qs] = alphas[idx] * acc_ref[idx, :, qs] + _dot(vt, probs[idx])

    m_ref[...] = jnp.full(m_ref.shape, -jnp.inf, F32)
    l_ref[...] = jnp.zeros(l_ref.shape, F32)
    acc_ref[...] = jnp.zeros(acc_ref.shape, F32)

    scores(0, sa_ref)

    def body(p, carry):
        j = 2 * p
        scores(j + 1, sb_ref)
        consume(j, sa_ref)
        scores(j + 2, sa_ref)
        consume(j + 1, sb_ref)
        return carry

    lax.fori_loop(0, qi, body, 0)
    triangle = (lax.broadcasted_iota(jnp.int32, (tk, tk), 0)
                <= lax.broadcasted_iota(jnp.int32, (tk, tk), 1))
    scores(2 * qi + 1, sb_ref, late_queries)
    consume(2 * qi, sa_ref, krow <= kcol)
    consume(2 * qi + 1, sb_ref, triangle, late_queries)

    lam = (jnp.exp(jnp.sum(lq1_ref[...] * lk1_ref[...], axis=-1, keepdims=True))
           - jnp.exp(jnp.sum(lq2_ref[...] * lk2_ref[...], axis=-1, keepdims=True))
           + LAMBDA_INIT)
    for g in range(heads):
        o = (acc_ref[2 * g] * (1.0 / l_ref[2 * g])
             - acc_ref[2 * g + 1] * (lam / l_ref[2 * g + 1]))
        ms = jnp.mean(o * o, axis=0, keepdims=True)
        o = (o * lax.rsqrt(ms + EPS)).T * sub_ref[...] * (1.0 - LAMBDA_INIT)
        o_ref[:, g * DIFF_V_DIM:(g + 1) * DIFF_V_DIM] = o.astype(o_ref.dtype)


def _diff_attn(u, lq1, lk1, lq2, lk2, subln, *, batch, seq, tk, heads):
    t = u.shape[0]
    blk = 2 * tk
    nq = seq // blk
    ng = DIFF_HEADS // heads
    assert DIFF_HEADS % heads == 0 and seq % blk == 0
    v_off = 4 * DIFF_HEADS * QK_DIM // (heads * DIFF_V_DIM)
    lam_spec = pl.BlockSpec((1, QK_DIM), lambda b, h, q: (0, 0))
    return pl.pallas_call(
        functools.partial(_diff_attn_kernel, tk=tk, heads=heads),
        grid=(batch, ng, nq),
        in_specs=[
            pl.BlockSpec((blk, heads * QK_DIM), lambda b, h, q: (b * nq + q, h)),
            pl.BlockSpec((blk, heads * QK_DIM), lambda b, h, q: (b * nq + q, ng + h)),
            pl.BlockSpec((seq, heads * QK_DIM), lambda b, h, q: (b, 2 * ng + h)),
            pl.BlockSpec((seq, heads * QK_DIM), lambda b, h, q: (b, 3 * ng + h)),
            pl.BlockSpec((seq, heads * DIFF_V_DIM), lambda b, h, q: (b, v_off + h)),
            lam_spec, lam_spec, lam_spec, lam_spec,
            pl.BlockSpec((1, DIFF_V_DIM), lambda b, h, q: (0, 0)),
        ],
        out_specs=pl.BlockSpec((blk, heads * DIFF_V_DIM), lambda b, h, q: (b * nq + q, h)),
        out_shape=jax.ShapeDtypeStruct((t, DIFF_HEADS * DIFF_V_DIM), BF16),
        scratch_shapes=[
            pltpu.VMEM((heads * DIFF_V_DIM, seq), BF16),
            pltpu.VMEM((2 * heads, tk, blk), F32),
            pltpu.VMEM((2 * heads, tk, blk), F32),
            pltpu.VMEM((2 * heads, 1, blk), F32),
            pltpu.VMEM((2 * heads, 1, blk), F32),
            pltpu.VMEM((2 * heads, DIFF_V_DIM, blk), F32),
        ],
        compiler_params=_params("parallel", "parallel", "arbitrary"),
        name="diff_attn",
    )(u, u, u, u, u, lq1, lk1, lq2, lk2, subln)


def _sb_attn_kernel(q_ref, k_ref, v_ref, o_ref, vt_ref, up_ref, sa_ref, sb_ref, c_ref, acc_ref,
                    *, tk, heads):
    qi = pl.program_id(2)
    tq = 2 * tk
    krow = lax.broadcasted_iota(jnp.int32, (tk, tq), 0)
    kcol = lax.broadcasted_iota(jnp.int32, (tk, tq), 1)

    @pl.when(qi == 0)
    def _():
        vt_ref[...] = v_ref[...].astype(F32).T.astype(BF16)
        r = lax.broadcasted_iota(jnp.int32, (tk, tk), 0)
        c = lax.broadcasted_iota(jnp.int32, (tk, tk), 1)
        upper = (c >= r).astype(BF16)
        upper = jnp.concatenate([upper, upper], axis=1)
        up_ref[...] = jnp.concatenate([upper, jnp.ones((SUM_ROWS, 2 * tk), BF16)], axis=0)

    heads_hd = [slice(g * QK_DIM, (g + 1) * QK_DIM) for g in range(heads)]
    every_query = slice(0, tq)
    late_queries = slice(tk, tq)

    def scores(j, s_ref, qs=every_query):
        start = pl.multiple_of(j * tk, tk)
        width = qs.stop - qs.start
        for g, hd in enumerate(heads_hd):
            s_ref[g, :, :width] = _dot_nt(k_ref[pl.ds(start, tk), hd], q_ref[qs, hd])

    def consume(j, s_ref, visible=None, qs=every_query):
        start = pl.multiple_of(j * tk, tk)
        width = qs.stop - qs.start
        logits, hilo = [], []
        for g in range(heads):
            z = s_ref[g, :, :width]
            if visible is not None:
                z = jnp.where(visible, z, -MASKED_LOGIT)
            sp = jnp.maximum(z, 0.0) + jnp.log2(1.0 + jnp.exp2(-jnp.abs(z)))
            logits.append(z - c_ref[g, 0:1, qs])
            hi = sp.astype(BF16)
            lo = (sp - hi.astype(F32)).astype(BF16)
            hilo.append(jnp.concatenate([hi, lo], axis=0))
        sums = [_dot(up_ref[...], x) for x in hilo]
        weights = []
        for g in range(heads):
            weights.append(jnp.exp2(logits[g] - sums[g][:tk]).astype(BF16))
            c_ref[g, :, qs] += sums[g][tk:tk + C_ROWS]
        for g, hd in enumerate(heads_hd):
            acc_ref[g, :, qs] += _dot(vt_ref[hd, pl.ds(start, tk)], weights[g])

    c_ref[...] = jnp.zeros(c_ref.shape, F32)
    acc_ref[...] = jnp.zeros(acc_ref.shape, F32)

    last = 2 * qi + 1

    def block(i):
        return jnp.maximum(last - i, 0)

    triangle = (lax.broadcasted_iota(jnp.int32, (tk, tk), 0)
                < lax.broadcasted_iota(jnp.int32, (tk, tk), 1))
    scores(last, sa_ref, late_queries)
    scores(block(1), sb_ref)
    consume(last, sa_ref, triangle, late_queries)
    scores(block(2), sa_ref)
    consume(block(1), sb_ref, krow < kcol)

    def body(p, carry):
        i = 2 + 2 * p
        scores(block(i + 1), sb_ref)
        consume(block(i), sa_ref)
        scores(block(i + 2), sa_ref)
        consume(block(i + 1), sb_ref)
        return carry

    lax.fori_loop(0, qi, body, 0)

    for g in range(heads):
        o_ref[:, g * QK_DIM:(g + 1) * QK_DIM] = acc_ref[g].T.astype(o_ref.dtype)


def _sb_attn(u, *, batch, seq, tk, heads):
    t = u.shape[0]
    blk = 2 * tk
    nq = seq // blk
    ng = SB_HEADS // heads
    assert SB_HEADS % heads == 0 and seq % blk == 0
    w = heads * QK_DIM
    off = (4 * DIFF_HEADS * QK_DIM + DIFF_HEADS * DIFF_V_DIM) // w
    return pl.pallas_call(
        functools.partial(_sb_attn_kernel, tk=tk, heads=heads),
        grid=(batch, ng, nq),
        in_specs=[
            pl.BlockSpec((blk, w), lambda b, h, q: (b * nq + q, off + h)),
            pl.BlockSpec((seq, w), lambda b, h, q: (b, off + ng + h)),
            pl.BlockSpec((seq, w), lambda b, h, q: (b, off + 2 * ng + h)),
        ],
        out_specs=pl.BlockSpec((blk, w), lambda b, h, q: (b * nq + q, h)),
        out_shape=jax.ShapeDtypeStruct((t, SB_HEADS * QK_DIM), BF16),
        scratch_shapes=[
            pltpu.VMEM((w, seq), BF16),
            pltpu.VMEM((tk + SUM_ROWS, 2 * tk), BF16),
            pltpu.VMEM((heads, tk, blk), F32),
            pltpu.VMEM((heads, tk, blk), F32),
            pltpu.VMEM((heads, C_ROWS, blk), F32),
            pltpu.VMEM((heads, QK_DIM, blk), F32),
        ],
        compiler_params=_params("parallel", "parallel", "arbitrary"),
        name="sb_attn",
    )(u, u, u)


def _out_proj_kernel(x_ref, a_ref, b_ref, wa_ref, wb_ref, o_ref):
    o_ref[...] = x_ref[...] + _dot(a_ref[...], wa_ref[...]) + _dot(b_ref[...], wb_ref[...])


def _out_proj(x, a, b, w, *, tm):
    t, d = x.shape
    wa_rows, wb_rows = a.shape[1], b.shape[1]
    assert t % tm == 0 and wa_rows == wb_rows and w.shape[0] == wa_rows + wb_rows
    return pl.pallas_call(
        _out_proj_kernel,
        grid=(t // tm,),
        in_specs=[
            pl.BlockSpec((tm, d), lambda i: (i, 0)),
            pl.BlockSpec((tm, wa_rows), lambda i: (i, 0)),
            pl.BlockSpec((tm, wb_rows), lambda i: (i, 0)),
            pl.BlockSpec((wa_rows, d), lambda i: (0, 0)),
            pl.BlockSpec((wb_rows, d), lambda i: (1, 0)),
        ],
        out_specs=pl.BlockSpec((tm, d), lambda i: (i, 0)),
        out_shape=jax.ShapeDtypeStruct((t, d), F32),
        compiler_params=_params("parallel"),
        name="out_proj",
    )(x, a, b, w, w)


def _rope_tables(seq):
    half = ROPE_DIM // 2
    pos = jnp.arange(seq, dtype=F32)
    inv_freq = ROPE_THETA ** (-jnp.arange(0, ROPE_DIM, 2, dtype=F32) / ROPE_DIM)
    ang = pos[:, None] * inv_freq[None, :]
    cos, sin = jnp.cos(ang), jnp.sin(ang)
    ones = jnp.ones((seq, QK_DIM - ROPE_DIM), F32)
    zeros_h = jnp.zeros((seq, half), F32)
    zeros_r = jnp.zeros((seq, QK_DIM - ROPE_DIM), F32)
    cos_t = jnp.concatenate([cos, cos, ones], axis=1)
    sa_t = jnp.concatenate([-sin, zeros_h, zeros_r], axis=1)
    sb_t = jnp.concatenate([zeros_h, sin, zeros_r], axis=1)
    return cos_t, sa_t, sb_t


def _pick(n, pref):
    for c in pref:
        if n % c == 0:
            return c
    return n


def kernel(x, ffn1_norm, ffn1_w_gate, ffn1_w_up, ffn1_w_down, mix_norm, w_in, q_norm, k_norm,
           lambda_q1, lambda_k1, lambda_q2, lambda_k2, subln, w_out,
           ffn2_norm, ffn2_w_gate, ffn2_w_up, ffn2_w_down, final_norm):
    batch, seq, d = x.shape
    assert ffn1_norm.shape[0] == 1, "single-layer block"
    t = batch * seq
    l = 0
    dff = ffn1_w_gate.shape[-1]
    tm = _pick(t, (1024, 512, 256, 128))
    tf = _pick(dff, (256, 128))
    tk = _pick(seq // 2, (256, 128))
    bf = lambda w: w[l].astype(BF16)
    mat = lambda w: w.reshape(w.shape[1:])
    row = lambda g: g[l].astype(F32)[None, :]

    xt = x.reshape(t, d)
    x1 = _ffn(xt, row(ffn1_norm), mat(ffn1_w_gate), mat(ffn1_w_up), mat(ffn1_w_down),
              row(final_norm), final_norm=False, tm=tm, tf=tf)

    cos_t, sa_t, sb_t = _rope_tables(seq)
    u = _in_proj(x1, row(mix_norm), bf(w_in), row(q_norm), row(k_norm), cos_t, sa_t, sb_t,
                 seq=seq, tm=_pick(seq, (1024, 512, 256, 128)), tn=512)

    a = _diff_attn(u, row(lambda_q1), row(lambda_k1), row(lambda_q2), row(lambda_k2),
                   row(subln), batch=batch, seq=seq, tk=tk, heads=4)
    b = _sb_attn(u, batch=batch, seq=seq, tk=tk, heads=4)

    x2 = _out_proj(x1, a, b, bf(w_out), tm=_pick(t, (512, 256, 128)))

    out = _ffn(x2, row(ffn2_norm), mat(ffn2_w_gate), mat(ffn2_w_up), mat(ffn2_w_down),
               row(final_norm), final_norm=True, tm=tm, tf=tf)
    return out.reshape(batch, seq, d)
```

```python
import functools
import math

import jax
import jax.numpy as jnp
from jax import lax
from jax.experimental import pallas as pl
from jax.experimental.pallas import tpu as pltpu

EPS = 1e-5
ROPE_THETA = 500000.0
LAMBDA_INIT = 0.8 - 0.6 * math.exp(-0.3 * 0)
LOG2E = math.log2(math.e)
MASKED_LOGIT = 1e30
UNDERFLOW_LOG2 = 160.0

DIFF_HEADS = 4
SB_HEADS = 8
QK_DIM = 128
DIFF_V_DIM = 2 * QK_DIM
ROPE_DIM = QK_DIM // 4

F32 = jnp.float32
BF16 = jnp.bfloat16

VMEM_LIMIT_BYTES = 56 * 1024 * 1024
ROW_CHUNK = 256
SUM_ROWS = 16
C_ROWS = 8


def _params(*sem):
    return pltpu.CompilerParams(dimension_semantics=sem, vmem_limit_bytes=VMEM_LIMIT_BYTES)


def _rms(x, g):
    ms = jnp.mean(x * x, axis=-1, keepdims=True)
    return x * lax.rsqrt(ms + EPS) * g


def _dot(a, b):
    return jnp.dot(a, b, preferred_element_type=F32)


def _dot_nt(a, b):
    return lax.dot_general(a, b, (((1,), (1,)), ((), ())), preferred_element_type=F32)


def _ffn_kernel(x_ref, g_ref, wg_ref, wu_ref, wd_ref, fg_ref, o_ref, xn_ref, *, final_norm):
    f = pl.program_id(1)

    @pl.when(f == 0)
    def _():
        x = x_ref[...]
        xn_ref[...] = _rms(x, g_ref[...]).astype(BF16)
        o_ref[...] = x

    xn = xn_ref[...]
    gate = _dot(xn, wg_ref[...].astype(BF16))
    up = _dot(xn, wu_ref[...].astype(BF16))
    h = (gate * jax.nn.sigmoid(gate)) * (up * 0.5)
    o_ref[...] += _dot(h.astype(BF16), wd_ref[...].astype(BF16))

    if final_norm:
        @pl.when(f == pl.num_programs(1) - 1)
        def _():
            o_ref[...] = _rms(o_ref[...], fg_ref[...])


def _ffn(x, norm_g, wg, wu, wd, final_g, *, final_norm, tm, tf):
    t, d = x.shape
    dff = wg.shape[1]
    assert t % tm == 0 and dff % tf == 0
    return pl.pallas_call(
        functools.partial(_ffn_kernel, final_norm=final_norm),
        grid=(t // tm, dff // tf),
        in_specs=[
            pl.BlockSpec((tm, d), lambda i, f: (i, 0)),
            pl.BlockSpec((1, d), lambda i, f: (0, 0)),
            pl.BlockSpec((d, tf), lambda i, f: (0, f)),
            pl.BlockSpec((d, tf), lambda i, f: (0, f)),
            pl.BlockSpec((tf, d), lambda i, f: (f, 0)),
            pl.BlockSpec((1, d), lambda i, f: (0, 0)),
        ],
        out_specs=pl.BlockSpec((tm, d), lambda i, f: (i, 0)),
        out_shape=jax.ShapeDtypeStruct((t, d), F32),
        scratch_shapes=[pltpu.VMEM((tm, d), BF16)],
        compiler_params=_params("parallel", "arbitrary"),
        name="ffn_final" if final_norm else "ffn",
    )(x, norm_g, wg, wu, wd, final_g)


def _in_proj_kernel(x_ref, g_ref, w_ref, qg_ref, kg_ref, cos_ref, sa_ref, sb_ref,
                    o_ref, xn_ref, *, tn, n_qk_tiles, sbq_lo, sbq_hi, scale):
    j = pl.program_id(1)

    tm = o_ref.shape[0]
    chunk = min(tm, ROW_CHUNK)

    def qk_tile(first):
        w = w_ref[:, pl.ds(pl.multiple_of(j * tn, tn), tn)]
        is_q = j < n_qk_tiles // 2
        gain = jnp.where(is_q, qg_ref[...], kg_ref[...])
        post = jnp.where(is_q, scale, 1.0).astype(F32)
        for r in range(tm // chunk):
            rows = slice(r * chunk, (r + 1) * chunk)
            if first:
                xn = _rms(x_ref[rows, :], g_ref[...]).astype(BF16)
                xn_ref[rows, :] = xn
            else:
                xn = xn_ref[rows, :]
            u = _dot(xn, w)
            cos, sa, sb = cos_ref[rows, :], sa_ref[rows, :], sb_ref[rows, :]
            for c in range(tn // QK_DIM):
                t = u[:, c * QK_DIM:(c + 1) * QK_DIM]
                ms = jnp.mean(t * t, axis=-1, keepdims=True)
                t = t * (lax.rsqrt(ms + EPS) * post) * gain
                t = (t * cos + pltpu.roll(t, QK_DIM - ROPE_DIM // 2, 1) * sa
                     + pltpu.roll(t, ROPE_DIM // 2, 1) * sb)
                o_ref[rows, c * QK_DIM:(c + 1) * QK_DIM] = t.astype(BF16)

    @pl.when(j == 0)
    def _():
        qk_tile(True)

    @pl.when((j > 0) & (j < n_qk_tiles))
    def _():
        qk_tile(False)

    @pl.when(j >= n_qk_tiles)
    def _():
        post = jnp.where((j >= sbq_lo) & (j < sbq_hi), scale, 1.0).astype(F32)
        u = _dot(xn_ref[...], w_ref[:, pl.ds(pl.multiple_of(j * tn, tn), tn)])
        o_ref[...] = (u * post).astype(BF16)


def _in_proj(x, norm_g, w, qg, kg, cos_t, sa_t, sb_t, *, seq, tm, tn):
    t, d = x.shape
    n = w.shape[1]
    diff_qk_cols = DIFF_HEADS * QK_DIM
    assert t % tm == 0 and seq % tm == 0 and n % tn == 0 and diff_qk_cols % tn == 0
    n_qk_tiles = 4 * diff_qk_cols // tn
    sbq_lo = (4 * diff_qk_cols + DIFF_HEADS * DIFF_V_DIM) // tn
    sbq_hi = sbq_lo + SB_HEADS * QK_DIM // tn
    s_blocks = seq // tm
    rope_spec = pl.BlockSpec((tm, QK_DIM), lambda i, j: (i % s_blocks, 0))
    kern = functools.partial(_in_proj_kernel, tn=tn, n_qk_tiles=n_qk_tiles,
                             sbq_lo=sbq_lo, sbq_hi=sbq_hi, scale=QK_DIM ** -0.5 * LOG2E)
    return pl.pallas_call(
        kern,
        grid=(t // tm, n // tn),
        in_specs=[
            pl.BlockSpec((tm, d), lambda i, j: (i, 0)),
            pl.BlockSpec((1, d), lambda i, j: (0, 0)),
            pl.BlockSpec((d, n), lambda i, j: (0, 0), pipeline_mode=pl.Buffered(1)),
            pl.BlockSpec((1, QK_DIM), lambda i, j: (0, 0)),
            pl.BlockSpec((1, QK_DIM), lambda i, j: (0, 0)),
            rope_spec, rope_spec, rope_spec,
        ],
        out_specs=pl.BlockSpec((tm, tn), lambda i, j: (i, j)),
        out_shape=jax.ShapeDtypeStruct((t, n), BF16),
        scratch_shapes=[pltpu.VMEM((tm, d), BF16)],
        compiler_params=_params("parallel", "arbitrary"),
        name="in_proj",
    )(x, norm_g, w, qg, kg, cos_t, sa_t, sb_t)


def _diff_attn_kernel(q1_ref, q2_ref, k1_ref, k2_ref, v_ref, lq1_ref, lk1_ref, lq2_ref,
                      lk2_ref, sub_ref, o_ref, vt_ref, sa_ref, sb_ref, m_ref, l_ref, acc_ref,
                      *, tk, heads):
    qi = pl.program_id(2)
    tq = 2 * tk

    @pl.when(qi == 0)
    def _():
        vt_ref[...] = v_ref[...].astype(F32).T.astype(BF16)

    q_refs = (q1_ref, q2_ref)
    k_refs = (k1_ref, k2_ref)
    krow = lax.broadcasted_iota(jnp.int32, (tk, tq), 0)
    kcol = lax.broadcasted_iota(jnp.int32, (tk, tq), 1)

    chains = [(g, mp) for g in range(heads) for mp in range(2)]

    every_query = slice(0, tq)
    late_queries = slice(tk, tq)

    def scores(j, s_ref, qs=every_query):
        start = pl.multiple_of(j * tk, tk)
        width = qs.stop - qs.start
        for idx, (g, mp) in enumerate(chains):
            hd = slice(g * QK_DIM, (g + 1) * QK_DIM)
            s_ref[idx, :, :width] = _dot_nt(k_refs[mp][pl.ds(start, tk), hd], q_refs[mp][qs, hd])

    def consume(j, s_ref, visible=None, qs=every_query):
        start = pl.multiple_of(j * tk, tk)
        width = qs.stop - qs.start
        probs, alphas = [], []
        for idx in range(len(chains)):
            s = s_ref[idx, :, :width]
            if visible is not None:
                s = jnp.where(visible, s, -jnp.inf)
            m_old = m_ref[idx, :, qs]
            m = jnp.maximum(m_old, jnp.max(s, axis=0, keepdims=True))
            alpha = jnp.exp2(m_old - m)
            p = jnp.exp2(s - m)
            l_ref[idx, :, qs] = alpha * l_ref[idx, :, qs] + jnp.sum(p, axis=0, keepdims=True)
            m_ref[idx, :, qs] = m
            alphas.append(alpha)
            probs.append(p.astype(BF16))
        for idx, (g, mp) in enumerate(chains):
            vt = vt_ref[g * DIFF_V_DIM:(g + 1) * DIFF_V_DIM, pl.ds(start, tk)]
            acc_ref[idx, :, qs] = alphas[idx] * acc_ref[idx, :, qs] + _dot(vt, probs[idx])

    m_ref[...] = jnp.full(m_ref.shape, -jnp.inf, F32)
    l_ref[...] = jnp.zeros(l_ref.shape, F32)
    acc_ref[...] = jnp.zeros(acc_ref.shape, F32)

    scores(0, sa_ref)

    def body(p, carry):
        j = 2 * p
        scores(j + 1, sb_ref)
        consume(j, sa_ref)
        scores(j + 2, sa_ref)
        consume(j + 1, sb_ref)
        return carry

    lax.fori_loop(0, qi, body, 0)
    triangle = (lax.broadcasted_iota(jnp.int32, (tk, tk), 0)
                <= lax.broadcasted_iota(jnp.int32, (tk, tk), 1))
    scores(2 * qi + 1, sb_ref, late_queries)
    consume(2 * qi, sa_ref, krow <= kcol)
    consume(2 * qi + 1, sb_ref, triangle, late_queries)

    lam = (jnp.exp(jnp.sum(lq1_ref[...] * lk1_ref[...], axis=-1, keepdims=True))
           - jnp.exp(jnp.sum(lq2_ref[...] * lk2_ref[...], axis=-1, keepdims=True))
           + LAMBDA_INIT)
    for g in range(heads):
        o = (acc_ref[2 * g] * (1.0 / l_ref[2 * g])
             - acc_ref[2 * g + 1] * (lam / l_ref[2 * g + 1]))
        ms = jnp.mean(o * o, axis=0, keepdims=True)
        o = (o * lax.rsqrt(ms + EPS)).T * sub_ref[...] * (1.0 - LAMBDA_INIT)
        o_ref[:, g * DIFF_V_DIM:(g + 1) * DIFF_V_DIM] = o.astype(o_ref.dtype)


def _diff_attn(u, lq1, lk1, lq2, lk2, subln, *, batch, seq, tk, heads):
    t = u.shape[0]
    blk = 2 * tk
    nq = seq // blk
    ng = DIFF_HEADS // heads
    assert DIFF_HEADS % heads == 0 and seq % blk == 0
    v_off = 4 * DIFF_HEADS * QK_DIM // (heads * DIFF_V_DIM)
    lam_spec = pl.BlockSpec((1, QK_DIM), lambda b, h, q: (0, 0))
    return pl.pallas_call(
        functools.partial(_diff_attn_kernel, tk=tk, heads=heads),
        grid=(batch, ng, nq),
        in_specs=[
            pl.BlockSpec((blk, heads * QK_DIM), lambda b, h, q: (b * nq + q, h)),
            pl.BlockSpec((blk, heads * QK_DIM), lambda b, h, q: (b * nq + q, ng + h)),
            pl.BlockSpec((seq, heads * QK_DIM), lambda b, h, q: (b, 2 * ng + h)),
            pl.BlockSpec((seq, heads * QK_DIM), lambda b, h, q: (b, 3 * ng + h)),
            pl.BlockSpec((seq, heads * DIFF_V_DIM), lambda b, h, q: (b, v_off + h)),
            lam_spec, lam_spec, lam_spec, lam_spec,
            pl.BlockSpec((1, DIFF_V_DIM), lambda b, h, q: (0, 0)),
        ],
        out_specs=pl.BlockSpec((blk, heads * DIFF_V_DIM), lambda b, h, q: (b * nq + q, h)),
        out_shape=jax.ShapeDtypeStruct((t, DIFF_HEADS * DIFF_V_DIM), BF16),
        scratch_shapes=[
            pltpu.VMEM((heads * DIFF_V_DIM, seq), BF16),
            pltpu.VMEM((2 * heads, tk, blk), F32),
            pltpu.VMEM((2 * heads, tk, blk), F32),
            pltpu.VMEM((2 * heads, 1, blk), F32),
            pltpu.VMEM((2 * heads, 1, blk), F32),
            pltpu.VMEM((2 * heads, DIFF_V_DIM, blk), F32),
        ],
        compiler_params=_params("parallel", "parallel", "arbitrary"),
        name="diff_attn",
    )(u, u, u, u, u, lq1, lk1, lq2, lk2, subln)


def _sb_attn_kernel(q_ref, k_ref, v_ref, o_ref, vt_ref, up_ref, sa_ref, sb_ref, c_ref, acc_ref,
                    *, tk, heads):
    qi = pl.program_id(2)
    tq = 2 * tk
    krow = lax.broadcasted_iota(jnp.int32, (tk, tq), 0)
    kcol = lax.broadcasted_iota(jnp.int32, (tk, tq), 1)

    @pl.when(qi == 0)
    def _():
        vt_ref[...] = v_ref[...].astype(F32).T.astype(BF16)
        r = lax.broadcasted_iota(jnp.int32, (tk, tk), 0)
        c = lax.broadcasted_iota(jnp.int32, (tk, tk), 1)
        upper = (c >= r).astype(BF16)
        upper = jnp.concatenate([upper, upper], axis=1)
        up_ref[...] = jnp.concatenate([upper, jnp.ones((SUM_ROWS, 2 * tk), BF16)], axis=0)

    heads_hd = [slice(g * QK_DIM, (g + 1) * QK_DIM) for g in range(heads)]
    every_query = slice(0, tq)
    late_queries = slice(tk, tq)

    def scores(j, s_ref, qs=every_query):
        start = pl.multiple_of(j * tk, tk)
        width = qs.stop - qs.start
        for g, hd in enumerate(heads_hd):
            s_ref[g, :, :width] = _dot_nt(k_ref[pl.ds(start, tk), hd], q_ref[qs, hd])

    def consume(j, s_ref, visible=None, qs=every_query):
        start = pl.multiple_of(j * tk, tk)
        width = qs.stop - qs.start
        logits, hilo = [], []
        for g in range(heads):
            z = s_ref[g, :, :width]
            if visible is not None:
                z = jnp.where(visible, z, -MASKED_LOGIT)
            sp = jnp.maximum(z, 0.0) + jnp.log2(1.0 + jnp.exp2(-jnp.abs(z)))
            logits.append(z - c_ref[g, 0:1, qs])
            hi = sp.astype(BF16)
            lo = (sp - hi.astype(F32)).astype(BF16)
            hilo.append(jnp.concatenate([hi, lo], axis=0))
        sums = [_dot(up_ref[...], x) for x in hilo]
        weights = []
        for g in range(heads):
            weights.append(jnp.exp2(logits[g] - sums[g][:tk]).astype(BF16))
            c_ref[g, :, qs] += sums[g][tk:tk + C_ROWS]
        for g, hd in enumerate(heads_hd):
            acc_ref[g, :, qs] += _dot(vt_ref[hd, pl.ds(start, tk)], weights[g])

    c_ref[...] = jnp.zeros(c_ref.shape, F32)
    acc_ref[...] = jnp.zeros(acc_ref.shape, F32)

    last = 2 * qi + 1

    def block(i):
        return jnp.maximum(last - i, 0)

    triangle = (lax.broadcasted_iota(jnp.int32, (tk, tk), 0)
                < lax.broadcasted_iota(jnp.int32, (tk, tk), 1))
    scores(last, sa_ref, late_queries)
    scores(block(1), sb_ref)
    consume(last, sa_ref, triangle, late_queries)
    scores(block(2), sa_ref)
    consume(block(1), sb_ref, krow < kcol)

    def more(p):
        return (p < qi) & (jnp.min(c_ref[:, 0:1, :]) < UNDERFLOW_LOG2)

    def body(p):
        i = 2 + 2 * p
        scores(block(i + 1), sb_ref)
        consume(block(i), sa_ref)
        scores(block(i + 2), sa_ref)
        consume(block(i + 1), sb_ref)
        return p + 1

    lax.while_loop(more, body, 0)

    for g in range(heads):
        o_ref[:, g * QK_DIM:(g + 1) * QK_DIM] = acc_ref[g].T.astype(o_ref.dtype)


def _sb_attn(u, *, batch, seq, tk, heads):
    t = u.shape[0]
    blk = 2 * tk
    nq = seq // blk
    ng = SB_HEADS // heads
    assert SB_HEADS % heads == 0 and seq % blk == 0
    w = heads * QK_DIM
    off = (4 * DIFF_HEADS * QK_DIM + DIFF_HEADS * DIFF_V_DIM) // w
    return pl.pallas_call(
        functools.partial(_sb_attn_kernel, tk=tk, heads=heads),
        grid=(batch, ng, nq),
        in_specs=[
            pl.BlockSpec((blk, w), lambda b, h, q: (b * nq + q, off + h)),
            pl.BlockSpec((seq, w), lambda b, h, q: (b, off + ng + h)),
            pl.BlockSpec((seq, w), lambda b, h, q: (b, off + 2 * ng + h)),
        ],
        out_specs=pl.BlockSpec((blk, w), lambda b, h, q: (b * nq + q, h)),
        out_shape=jax.ShapeDtypeStruct((t, SB_HEADS * QK_DIM), BF16),
        scratch_shapes=[
            pltpu.VMEM((w, seq), BF16),
            pltpu.VMEM((tk + SUM_ROWS, 2 * tk), BF16),
            pltpu.VMEM((heads, tk, blk), F32),
            pltpu.VMEM((heads, tk, blk), F32),
            pltpu.VMEM((heads, C_ROWS, blk), F32),
            pltpu.VMEM((heads, QK_DIM, blk), F32),
        ],
        compiler_params=_params("parallel", "parallel", "arbitrary"),
        name="sb_attn",
    )(u, u, u)


def _out_proj_kernel(x_ref, a_ref, b_ref, wa_ref, wb_ref, o_ref):
    o_ref[...] = x_ref[...] + _dot(a_ref[...], wa_ref[...]) + _dot(b_ref[...], wb_ref[...])


def _out_proj(x, a, b, w, *, tm):
    t, d = x.shape
    wa_rows, wb_rows = a.shape[1], b.shape[1]
    assert t % tm == 0 and wa_rows == wb_rows and w.shape[0] == wa_rows + wb_rows
    return pl.pallas_call(
        _out_proj_kernel,
        grid=(t // tm,),
        in_specs=[
            pl.BlockSpec((tm, d), lambda i: (i, 0)),
            pl.BlockSpec((tm, wa_rows), lambda i: (i, 0)),
            pl.BlockSpec((tm, wb_rows), lambda i: (i, 0)),
            pl.BlockSpec((wa_rows, d), lambda i: (0, 0)),
            pl.BlockSpec((wb_rows, d), lambda i: (1, 0)),
        ],
        out_specs=pl.BlockSpec((tm, d), lambda i: (i, 0)),
        out_shape=jax.ShapeDtypeStruct((t, d), F32),
        compiler_params=_params("parallel"),
        name="out_proj",
    )(x, a, b, w, w)


def _rope_tables(seq):
    half = ROPE_DIM // 2
    pos = jnp.arange(seq, dtype=F32)
    inv_freq = ROPE_THETA ** (-jnp.arange(0, ROPE_DIM, 2, dtype=F32) / ROPE_DIM)
    ang = pos[:, None] * inv_freq[None, :]
    cos, sin = jnp.cos(ang), jnp.sin(ang)
    ones = jnp.ones((seq, QK_DIM - ROPE_DIM), F32)
    zeros_h = jnp.zeros((seq, half), F32)
    zeros_r = jnp.zeros((seq, QK_DIM - ROPE_DIM), F32)
    cos_t = jnp.concatenate([cos, cos, ones], axis=1)
    sa_t = jnp.concatenate([-sin, zeros_h, zeros_r], axis=1)
    sb_t = jnp.concatenate([zeros_h, sin, zeros_r], axis=1)
    return cos_t, sa_t, sb_t


def _pick(n, pref):
    for c in pref:
        if n % c == 0:
            return c
    return n


def kernel(x, ffn1_norm, ffn1_w_gate, ffn1_w_up, ffn1_w_down, mix_norm, w_in, q_norm, k_norm,
           lambda_q1, lambda_k1, lambda_q2, lambda_k2, subln, w_out,
           ffn2_norm, ffn2_w_gate, ffn2_w_up, ffn2_w_down, final_norm):
    batch, seq, d = x.shape
    assert ffn1_norm.shape[0] == 1, "single-layer block"
    t = batch * seq
    l = 0
    dff = ffn1_w_gate.shape[-1]
    tm = _pick(t, (1024, 512, 256, 128))
    tf = _pick(dff, (256, 128))
    tk = _pick(seq // 2, (256, 128))
    bf = lambda w: w[l].astype(BF16)
    mat = lambda w: w.reshape(w.shape[1:])
    row = lambda g: g[l].astype(F32)[None, :]

    xt = x.reshape(t, d)
    x1 = _ffn(xt, row(ffn1_norm), mat(ffn1_w_gate), mat(ffn1_w_up), mat(ffn1_w_down),
              row(final_norm), final_norm=False, tm=tm, tf=tf)

    cos_t, sa_t, sb_t = _rope_tables(seq)
    u = _in_proj(x1, row(mix_norm), bf(w_in), row(q_norm), row(k_norm), cos_t, sa_t, sb_t,
                 seq=seq, tm=_pick(seq, (1024, 512, 256, 128)), tn=512)

    a = _diff_attn(u, row(lambda_q1), row(lambda_k1), row(lambda_q2), row(lambda_k2),
                   row(subln), batch=batch, seq=seq, tk=tk, heads=4)
    b = _sb_attn(u, batch=batch, seq=seq, tk=tk, heads=4)

    x2 = _out_proj(x1, a, b, bf(w_out), tm=_pick(t, (512, 256, 128)))

    out = _ffn(x2, row(ffn2_norm), mat(ffn2_w_gate), mat(ffn2_w_up), mat(ffn2_w_down),
               row(final_norm), final_norm=True, tm=tm, tf=tf)
    return out.reshape(batch, seq, d)
```

```python
import functools
import math

import jax
import jax.numpy as jnp
from jax import lax
from jax.experimental import pallas as pl
from jax.experimental.pallas import tpu as pltpu

EPS = 1e-5
ROPE_THETA = 500000.0
LAMBDA_INIT = 0.8 - 0.6 * math.exp(-0.3 * 0)
LOG2E = math.log2(math.e)
MASKED_LOGIT = 1e30
UNDERFLOW_LOG2 = 160.0

DIFF_HEADS = 4
SB_HEADS = 8
QK_DIM = 128
DIFF_V_DIM = 2 * QK_DIM
ROPE_DIM = QK_DIM // 4

F32 = jnp.float32
BF16 = jnp.bfloat16

V7X_VMEM_BYTES = 64 * 1024 * 1024
VMEM_LIMIT_BYTES = V7X_VMEM_BYTES - 4 * 1024 * 1024
ROW_CHUNK = 256
SUM_ROWS = 16
C_ROWS = 8


def _params(*sem):
    return pltpu.CompilerParams(dimension_semantics=sem, vmem_limit_bytes=VMEM_LIMIT_BYTES)


def _rms(x, g):
    ms = jnp.mean(x * x, axis=-1, keepdims=True)
    return x * lax.rsqrt(ms + EPS) * g


def _dot(a, b):
    return jnp.dot(a, b, preferred_element_type=F32)


def _dot_nt(a, b):
    return lax.dot_general(a, b, (((1,), (1,)), ((), ())), preferred_element_type=F32)


def _ffn_kernel(x_ref, g_ref, wg_ref, wu_ref, wd_ref, fg_ref, o_ref, xn_ref, *, final_norm):
    f = pl.program_id(1)

    @pl.when(f == 0)
    def _():
        x = x_ref[...]
        xn_ref[...] = _rms(x, g_ref[...]).astype(BF16)
        o_ref[...] = x

    xn = xn_ref[...]
    gate = _dot(xn, wg_ref[...].astype(BF16))
    up = _dot(xn, wu_ref[...].astype(BF16))
    h = (gate * jax.nn.sigmoid(gate)) * (up * 0.5)
    o_ref[...] += _dot(h.astype(BF16), wd_ref[...].astype(BF16))

    if final_norm:
        @pl.when(f == pl.num_programs(1) - 1)
        def _():
            o_ref[...] = _rms(o_ref[...], fg_ref[...])


def _ffn(x, norm_g, wg, wu, wd, final_g, *, final_norm, tm, tf):
    t, d = x.shape
    dff = wg.shape[1]
    assert t % tm == 0 and dff % tf == 0
    return pl.pallas_call(
        functools.partial(_ffn_kernel, final_norm=final_norm),
        grid=(t // tm, dff // tf),
        in_specs=[
            pl.BlockSpec((tm, d), lambda i, f: (i, 0)),
            pl.BlockSpec((1, d), lambda i, f: (0, 0)),
            pl.BlockSpec((d, tf), lambda i, f: (0, f)),
            pl.BlockSpec((d, tf), lambda i, f: (0, f)),
            pl.BlockSpec((tf, d), lambda i, f: (f, 0)),
            pl.BlockSpec((1, d), lambda i, f: (0, 0)),
        ],
        out_specs=pl.BlockSpec((tm, d), lambda i, f: (i, 0)),
        out_shape=jax.ShapeDtypeStruct((t, d), F32),
        scratch_shapes=[pltpu.VMEM((tm, d), BF16)],
        compiler_params=_params("parallel", "arbitrary"),
        name="ffn_final" if final_norm else "ffn",
    )(x, norm_g, wg, wu, wd, final_g)


def _in_proj_kernel(x_ref, g_ref, w_ref, qg_ref, kg_ref, cos_ref, sa_ref, sb_ref,
                    o_ref, xn_ref, *, tn, n_qk_tiles, sbq_lo, sbq_hi, scale):
    j = pl.program_id(1)

    tm = o_ref.shape[0]
    chunk = min(tm, ROW_CHUNK)

    def qk_tile(first):
        w = w_ref[:, pl.ds(pl.multiple_of(j * tn, tn), tn)]
        is_q = j < n_qk_tiles // 2
        gain = jnp.where(is_q, qg_ref[...], kg_ref[...])
        post = jnp.where(is_q, scale, 1.0).astype(F32)
        for r in range(tm // chunk):
            rows = slice(r * chunk, (r + 1) * chunk)
            if first:
                xn = _rms(x_ref[rows, :], g_ref[...]).astype(BF16)
                xn_ref[rows, :] = xn
            else:
                xn = xn_ref[rows, :]
            u = _dot(xn, w)
            cos, sa, sb = cos_ref[rows, :], sa_ref[rows, :], sb_ref[rows, :]
            for c in range(tn // QK_DIM):
                t = u[:, c * QK_DIM:(c + 1) * QK_DIM]
                ms = jnp.mean(t * t, axis=-1, keepdims=True)
                t = t * (lax.rsqrt(ms + EPS) * post) * gain
                t = (t * cos + pltpu.roll(t, QK_DIM - ROPE_DIM // 2, 1) * sa
                     + pltpu.roll(t, ROPE_DIM // 2, 1) * sb)
                o_ref[rows, c * QK_DIM:(c + 1) * QK_DIM] = t.astype(BF16)

    @pl.when(j == 0)
    def _():
        qk_tile(True)

    @pl.when((j > 0) & (j < n_qk_tiles))
    def _():
        qk_tile(False)

    @pl.when(j >= n_qk_tiles)
    def _():
        post = jnp.where((j >= sbq_lo) & (j < sbq_hi), scale, 1.0).astype(F32)
        u = _dot(xn_ref[...], w_ref[:, pl.ds(pl.multiple_of(j * tn, tn), tn)])
        o_ref[...] = (u * post).astype(BF16)


def _in_proj(x, norm_g, w, qg, kg, cos_t, sa_t, sb_t, *, seq, tm, tn):
    t, d = x.shape
    n = w.shape[1]
    diff_qk_cols = DIFF_HEADS * QK_DIM
    assert t % tm == 0 and seq % tm == 0 and n % tn == 0 and (2 * diff_qk_cols) % tn == 0
    assert (DIFF_HEADS * DIFF_V_DIM) % tn == 0 and (SB_HEADS * QK_DIM) % tn == 0
    n_qk_tiles = 4 * diff_qk_cols // tn
    sbq_lo = (4 * diff_qk_cols + DIFF_HEADS * DIFF_V_DIM) // tn
    sbq_hi = sbq_lo + SB_HEADS * QK_DIM // tn
    s_blocks = seq // tm
    rope_spec = pl.BlockSpec((tm, QK_DIM), lambda i, j: (i % s_blocks, 0))
    kern = functools.partial(_in_proj_kernel, tn=tn, n_qk_tiles=n_qk_tiles,
                             sbq_lo=sbq_lo, sbq_hi=sbq_hi, scale=QK_DIM ** -0.5 * LOG2E)
    return pl.pallas_call(
        kern,
        grid=(t // tm, n // tn),
        in_specs=[
            pl.BlockSpec((tm, d), lambda i, j: (i, 0)),
            pl.BlockSpec((1, d), lambda i, j: (0, 0)),
            pl.BlockSpec((d, n), lambda i, j: (0, 0), pipeline_mode=pl.Buffered(1)),
            pl.BlockSpec((1, QK_DIM), lambda i, j: (0, 0)),
            pl.BlockSpec((1, QK_DIM), lambda i, j: (0, 0)),
            rope_spec, rope_spec, rope_spec,
        ],
        out_specs=pl.BlockSpec((tm, tn), lambda i, j: (i, j)),
        out_shape=jax.ShapeDtypeStruct((t, n), BF16),
        scratch_shapes=[pltpu.VMEM((tm, d), BF16)],
        compiler_params=_params("parallel", "arbitrary"),
        name="in_proj",
    )(x, norm_g, w, qg, kg, cos_t, sa_t, sb_t)


def _diff_attn_kernel(q1_ref, q2_ref, k1_ref, k2_ref, v_ref, lq1_ref, lk1_ref, lq2_ref,
                      lk2_ref, sub_ref, o_ref, vt_ref, sa_ref, sb_ref, m_ref, l_ref, acc_ref,
                      *, tk, heads):
    qi = pl.program_id(2)
    tq = 2 * tk

    @pl.when(qi == 0)
    def _():
        vt_ref[...] = v_ref[...].astype(F32).T.astype(BF16)

    q_refs = (q1_ref, q2_ref)
    k_refs = (k1_ref, k2_ref)
    krow = lax.broadcasted_iota(jnp.int32, (tk, tq), 0)
    kcol = lax.broadcasted_iota(jnp.int32, (tk, tq), 1)

    chains = [(g, mp) for g in range(heads) for mp in range(2)]

    every_query = slice(0, tq)
    late_queries = slice(tk, tq)

    def scores(j, s_ref, qs=every_query):
        start = pl.multiple_of(j * tk, tk)
        width = qs.stop - qs.start
        for idx, (g, mp) in enumerate(chains):
            hd = slice(g * QK_DIM, (g + 1) * QK_DIM)
            s_ref[idx, :, :width] = _dot_nt(k_refs[mp][pl.ds(start, tk), hd], q_refs[mp][qs, hd])

    def consume(j, s_ref, visible=None, qs=every_query):
        start = pl.multiple_of(j * tk, tk)
        width = qs.stop - qs.start
        probs, alphas = [], []
        for idx in range(len(chains)):
            s = s_ref[idx, :, :width]
            if visible is not None:
                s = jnp.where(visible, s, -jnp.inf)
            m_old = m_ref[idx, :, qs]
            m = jnp.maximum(m_old, jnp.max(s, axis=0, keepdims=True))
            alpha = jnp.exp2(m_old - m)
            p = jnp.exp2(s - m)
            l_ref[idx, :, qs] = alpha * l_ref[idx, :, qs] + jnp.sum(p, axis=0, keepdims=True)
            m_ref[idx, :, qs] = m
            alphas.append(alpha)
            probs.append(p.astype(BF16))
        for idx, (g, mp) in enumerate(chains):
            vt = vt_ref[g * DIFF_V_DIM:(g + 1) * DIFF_V_DIM, pl.ds(start, tk)]
            acc_ref[idx, :, qs] = alphas[idx] * acc_ref[idx, :, qs] + _dot(vt, probs[idx])

    m_ref[...] = jnp.full(m_ref.shape, -jnp.inf, F32)
    l_ref[...] = jnp.zeros(l_ref.shape, F32)
    acc_ref[...] = jnp.zeros(acc_ref.shape, F32)

    scores(0, sa_ref)

    def body(p, carry):
        j = 2 * p
        scores(j + 1, sb_ref)
        consume(j, sa_ref)
        scores(j + 2, sa_ref)
        consume(j + 1, sb_ref)
        return carry

    lax.fori_loop(0, qi, body, 0)
    triangle = (lax.broadcasted_iota(jnp.int32, (tk, tk), 0)
                <= lax.broadcasted_iota(jnp.int32, (tk, tk), 1))
    scores(2 * qi + 1, sb_ref, late_queries)
    consume(2 * qi, sa_ref, krow <= kcol)
    consume(2 * qi + 1, sb_ref, triangle, late_queries)

    lam = (jnp.exp(jnp.sum(lq1_ref[...] * lk1_ref[...], axis=-1, keepdims=True))
           - jnp.exp(jnp.sum(lq2_ref[...] * lk2_ref[...], axis=-1, keepdims=True))
           + LAMBDA_INIT)
    for g in range(heads):
        o = (acc_ref[2 * g] * (1.0 / l_ref[2 * g])
             - acc_ref[2 * g + 1] * (lam / l_ref[2 * g + 1]))
        ms = jnp.mean(o * o, axis=0, keepdims=True)
        o = (o * lax.rsqrt(ms + EPS)).T * sub_ref[...] * (1.0 - LAMBDA_INIT)
        o_ref[:, g * DIFF_V_DIM:(g + 1) * DIFF_V_DIM] = o.astype(o_ref.dtype)


def _diff_attn(u, lq1, lk1, lq2, lk2, subln, *, batch, seq, tk, heads):
    t = u.shape[0]
    blk = 2 * tk
    nq = seq // blk
    ng = DIFF_HEADS // heads
    assert DIFF_HEADS % heads == 0 and seq % blk == 0
    v_off = 4 * DIFF_HEADS * QK_DIM // (heads * DIFF_V_DIM)
    lam_spec = pl.BlockSpec((1, QK_DIM), lambda b, h, q: (0, 0))
    return pl.pallas_call(
        functools.partial(_diff_attn_kernel, tk=tk, heads=heads),
        grid=(batch, ng, nq),
        in_specs=[
            pl.BlockSpec((blk, heads * QK_DIM), lambda b, h, q: (b * nq + q, h)),
            pl.BlockSpec((blk, heads * QK_DIM), lambda b, h, q: (b * nq + q, ng + h)),
            pl.BlockSpec((seq, heads * QK_DIM), lambda b, h, q: (b, 2 * ng + h)),
            pl.BlockSpec((seq, heads * QK_DIM), lambda b, h, q: (b, 3 * ng + h)),
            pl.BlockSpec((seq, heads * DIFF_V_DIM), lambda b, h, q: (b, v_off + h)),
            lam_spec, lam_spec, lam_spec, lam_spec,
            pl.BlockSpec((1, DIFF_V_DIM), lambda b, h, q: (0, 0)),
        ],
        out_specs=pl.BlockSpec((blk, heads * DIFF_V_DIM), lambda b, h, q: (b * nq + q, h)),
        out_shape=jax.ShapeDtypeStruct((t, DIFF_HEADS * DIFF_V_DIM), BF16),
        scratch_shapes=[
            pltpu.VMEM((heads * DIFF_V_DIM, seq), BF16),
            pltpu.VMEM((2 * heads, tk, blk), F32),
            pltpu.VMEM((2 * heads, tk, blk), F32),
            pltpu.VMEM((2 * heads, 1, blk), F32),
            pltpu.VMEM((2 * heads, 1, blk), F32),
            pltpu.VMEM((2 * heads, DIFF_V_DIM, blk), F32),
        ],
        compiler_params=_params("parallel", "parallel", "arbitrary"),
        name="diff_attn",
    )(u, u, u, u, u, lq1, lk1, lq2, lk2, subln)


def _sb_attn_kernel(q_ref, k_ref, v_ref, o_ref, vt_ref, up_ref, sa_ref, sb_ref, c_ref, acc_ref,
                    *, tk, heads):
    qi = pl.program_id(2)
    tq = 2 * tk
    krow = lax.broadcasted_iota(jnp.int32, (tk, tq), 0)
    kcol = lax.broadcasted_iota(jnp.int32, (tk, tq), 1)

    @pl.when(qi == 0)
    def _():
        vt_ref[...] = v_ref[...].astype(F32).T.astype(BF16)
        r = lax.broadcasted_iota(jnp.int32, (tk, tk), 0)
        c = lax.broadcasted_iota(jnp.int32, (tk, tk), 1)
        upper = (c >= r).astype(BF16)
        upper = jnp.concatenate([upper, upper], axis=1)
        up_ref[...] = jnp.concatenate([upper, jnp.ones((SUM_ROWS, 2 * tk), BF16)], axis=0)

    heads_hd = [slice(g * QK_DIM, (g + 1) * QK_DIM) for g in range(heads)]
    every_query = slice(0, tq)
    late_queries = slice(tk, tq)

    def scores(j, s_ref, qs=every_query):
        start = pl.multiple_of(j * tk, tk)
        width = qs.stop - qs.start
        for g, hd in enumerate(heads_hd):
            s_ref[g, :, :width] = _dot_nt(k_ref[pl.ds(start, tk), hd], q_ref[qs, hd])

    def consume(j, s_ref, visible=None, qs=every_query):
        start = pl.multiple_of(j * tk, tk)
        width = qs.stop - qs.start
        logits, hilo = [], []
        for g in range(heads):
            z = s_ref[g, :, :width]
            if visible is not None:
                z = jnp.where(visible, z, -MASKED_LOGIT)
            sp = jnp.maximum(z, 0.0) + jnp.log2(1.0 + jnp.exp2(-jnp.abs(z)))
            logits.append(z - c_ref[g, 0:1, qs])
            hi = sp.astype(BF16)
            lo = (sp - hi.astype(F32)).astype(BF16)
            hilo.append(jnp.concatenate([hi, lo], axis=0))
        sums = [_dot(up_ref[...], x) for x in hilo]
        weights = []
        for g in range(heads):
            weights.append(jnp.exp2(logits[g] - sums[g][:tk]).astype(BF16))
            c_ref[g, :, qs] += sums[g][tk:tk + C_ROWS]
        for g, hd in enumerate(heads_hd):
            acc_ref[g, :, qs] += _dot(vt_ref[hd, pl.ds(start, tk)], weights[g])

    c_ref[...] = jnp.zeros(c_ref.shape, F32)
    acc_ref[...] = jnp.zeros(acc_ref.shape, F32)

    last = 2 * qi + 1

    def block(i):
        return jnp.maximum(last - i, 0)

    triangle = (lax.broadcasted_iota(jnp.int32, (tk, tk), 0)
                < lax.broadcasted_iota(jnp.int32, (tk, tk), 1))
    scores(last, sa_ref, late_queries)
    scores(block(1), sb_ref)
    consume(last, sa_ref, triangle, late_queries)
    scores(block(2), sa_ref)
    consume(block(1), sb_ref, krow < kcol)

    def weights_alive():
        return jnp.min(c_ref[:, 0:1, :]) < UNDERFLOW_LOG2

    def more(p):
        return (p < qi) & weights_alive()

    def body(p):
        i = 2 + 2 * p
        scores(block(i + 1), sb_ref)
        consume(block(i), sa_ref)

        @pl.when(weights_alive())
        def _():
            scores(block(i + 2), sa_ref)
            consume(block(i + 1), sb_ref)

        return p + 1

    lax.while_loop(more, body, 0)

    for g in range(heads):
        o_ref[:, g * QK_DIM:(g + 1) * QK_DIM] = acc_ref[g].T.astype(o_ref.dtype)


def _sb_attn(u, *, batch, seq, tk, heads):
    t = u.shape[0]
    blk = 2 * tk
    nq = seq // blk
    ng = SB_HEADS // heads
    assert SB_HEADS % heads == 0 and seq % blk == 0
    w = heads * QK_DIM
    off = (4 * DIFF_HEADS * QK_DIM + DIFF_HEADS * DIFF_V_DIM) // w
    return pl.pallas_call(
        functools.partial(_sb_attn_kernel, tk=tk, heads=heads),
        grid=(batch, ng, nq),
        in_specs=[
            pl.BlockSpec((blk, w), lambda b, h, q: (b * nq + q, off + h)),
            pl.BlockSpec((seq, w), lambda b, h, q: (b, off + ng + h)),
            pl.BlockSpec((seq, w), lambda b, h, q: (b, off + 2 * ng + h)),
        ],
        out_specs=pl.BlockSpec((blk, w), lambda b, h, q: (b * nq + q, h)),
        out_shape=jax.ShapeDtypeStruct((t, SB_HEADS * QK_DIM), BF16),
        scratch_shapes=[
            pltpu.VMEM((w, seq), BF16),
            pltpu.VMEM((tk + SUM_ROWS, 2 * tk), BF16),
            pltpu.VMEM((heads, tk, blk), F32),
            pltpu.VMEM((heads, tk, blk), F32),
            pltpu.VMEM((heads, C_ROWS, blk), F32),
            pltpu.VMEM((heads, QK_DIM, blk), F32),
        ],
        compiler_params=_params("parallel", "parallel", "arbitrary"),
        name="sb_attn",
    )(u, u, u)


def _out_proj_kernel(x_ref, a_ref, b_ref, wa_ref, wb_ref, o_ref):
    o_ref[...] = x_ref[...] + _dot(a_ref[...], wa_ref[...]) + _dot(b_ref[...], wb_ref[...])


def _out_proj(x, a, b, w, *, tm):
    t, d = x.shape
    wa_rows, wb_rows = a.shape[1], b.shape[1]
    assert t % tm == 0 and wa_rows == wb_rows and w.shape[0] == wa_rows + wb_rows
    return pl.pallas_call(
        _out_proj_kernel,
        grid=(t // tm,),
        in_specs=[
            pl.BlockSpec((tm, d), lambda i: (i, 0)),
            pl.BlockSpec((tm, wa_rows), lambda i: (i, 0)),
            pl.BlockSpec((tm, wb_rows), lambda i: (i, 0)),
            pl.BlockSpec((wa_rows, d), lambda i: (0, 0)),
            pl.BlockSpec((wb_rows, d), lambda i: (1, 0)),
        ],
        out_specs=pl.BlockSpec((tm, d), lambda i: (i, 0)),
        out_shape=jax.ShapeDtypeStruct((t, d), F32),
        compiler_params=_params("parallel"),
        name="out_proj",
    )(x, a, b, w, w)


def _rope_tables(seq):
    half = ROPE_DIM // 2
    pos = jnp.arange(seq, dtype=F32)
    inv_freq = ROPE_THETA ** (-jnp.arange(0, ROPE_DIM, 2, dtype=F32) / ROPE_DIM)
    ang = pos[:, None] * inv_freq[None, :]
    cos, sin = jnp.cos(ang), jnp.sin(ang)
    ones = jnp.ones((seq, QK_DIM - ROPE_DIM), F32)
    zeros_h = jnp.zeros((seq, half), F32)
    zeros_r = jnp.zeros((seq, QK_DIM - ROPE_DIM), F32)
    cos_t = jnp.concatenate([cos, cos, ones], axis=1)
    sa_t = jnp.concatenate([-sin, zeros_h, zeros_r], axis=1)
    sb_t = jnp.concatenate([zeros_h, sin, zeros_r], axis=1)
    return cos_t, sa_t, sb_t


def _pick(n, pref):
    for c in pref:
        if n % c == 0:
            return c
    return n


def kernel(x, ffn1_norm, ffn1_w_gate, ffn1_w_up, ffn1_w_down, mix_norm, w_in, q_norm, k_norm,
           lambda_q1, lambda_k1, lambda_q2, lambda_k2, subln, w_out,
           ffn2_norm, ffn2_w_gate, ffn2_w_up, ffn2_w_down, final_norm):
    batch, seq, d = x.shape
    assert ffn1_norm.shape[0] == 1, "single-layer block"
    t = batch * seq
    l = 0
    dff = ffn1_w_gate.shape[-1]
    tm = _pick(t, (1024, 512, 256, 128))
    tf = _pick(dff, (256, 128))
    tk = _pick(seq // 2, (256, 128))
    bf = lambda w: w[l].astype(BF16)
    mat = lambda w: w.reshape(w.shape[1:])
    row = lambda g: g[l].astype(F32)[None, :]

    xt = x.reshape(t, d)
    x1 = _ffn(xt, row(ffn1_norm), mat(ffn1_w_gate), mat(ffn1_w_up), mat(ffn1_w_down),
              row(final_norm), final_norm=False, tm=tm, tf=tf)

    cos_t, sa_t, sb_t = _rope_tables(seq)
    u = _in_proj(x1, row(mix_norm), bf(w_in), row(q_norm), row(k_norm), cos_t, sa_t, sb_t,
                 seq=seq, tm=_pick(seq, (1024, 512, 256, 128)), tn=1024)

    a = _diff_attn(u, row(lambda_q1), row(lambda_k1), row(lambda_q2), row(lambda_k2),
                   row(subln), batch=batch, seq=seq, tk=tk, heads=4)
    b = _sb_attn(u, batch=batch, seq=seq, tk=tk, heads=4)

    x2 = _out_proj(x1, a, b, bf(w_out), tm=_pick(t, (512, 256, 128)))

    out = _ffn(x2, row(ffn2_norm), mat(ffn2_w_gate), mat(ffn2_w_up), mat(ffn2_w_down),
               row(final_norm), final_norm=True, tm=tm, tf=tf)
    return out.reshape(batch, seq, d)
```

```python
import functools
import math

import jax
import jax.numpy as jnp
from jax import lax
from jax.experimental import pallas as pl
from jax.experimental.pallas import tpu as pltpu

EPS = 1e-5
ROPE_THETA = 500000.0
LAMBDA_INIT = 0.8 - 0.6 * math.exp(-0.3 * 0)
LOG2E = math.log2(math.e)
MASKED_LOGIT = 1e30
UNDERFLOW_LOG2 = 160.0

DIFF_HEADS = 4
SB_HEADS = 8
QK_DIM = 128
DIFF_V_DIM = 2 * QK_DIM
ROPE_DIM = QK_DIM // 4

F32 = jnp.float32
BF16 = jnp.bfloat16

V7X_VMEM_BYTES = 64 * 1024 * 1024
VMEM_LIMIT_BYTES = V7X_VMEM_BYTES - 4 * 1024 * 1024
ROW_CHUNK = 256
SUM_ROWS = 16
C_ROWS = 8


def _params(*sem):
    return pltpu.CompilerParams(dimension_semantics=sem, vmem_limit_bytes=VMEM_LIMIT_BYTES)


def _rms(x, g):
    ms = jnp.mean(x * x, axis=-1, keepdims=True)
    return x * lax.rsqrt(ms + EPS) * g


def _dot(a, b):
    return jnp.dot(a, b, preferred_element_type=F32)


def _dot_nt(a, b):
    return lax.dot_general(a, b, (((1,), (1,)), ((), ())), preferred_element_type=F32)


def _ffn_kernel(x_hbm, g_ref, wg_ref, wu_ref, wd_ref, fg_ref, o_ref, xn_ref, x_ref, x_sem,
                *, final_norm):
    i, f = pl.program_id(0), pl.program_id(1)
    tm = x_ref.shape[0]

    def x_copy(tile):
        return pltpu.make_async_copy(x_hbm.at[pl.ds(tile * tm, tm), :], x_ref, x_sem)

    @pl.when((i == 0) & (f == 0))
    def _():
        x_copy(0).start()

    @pl.when(f == 0)
    def _():
        x_copy(i).wait()
        x = x_ref[...]
        xn_ref[...] = _rms(x, g_ref[...]).astype(BF16)
        o_ref[...] = x

    @pl.when((f == 1) & (i + 1 < pl.num_programs(0)))
    def _():
        x_copy(i + 1).start()

    xn = xn_ref[...]
    gate = _dot(xn, wg_ref[...].astype(BF16))
    up = _dot(xn, wu_ref[...].astype(BF16))
    h = (gate * jax.nn.sigmoid(gate)) * (up * 0.5)
    o_ref[...] += _dot(h.astype(BF16), wd_ref[...].astype(BF16))

    if final_norm:
        @pl.when(f == pl.num_programs(1) - 1)
        def _():
            o_ref[...] = _rms(o_ref[...], fg_ref[...])


def _ffn(x, norm_g, wg, wu, wd, final_g, *, final_norm, tm, tf):
    t, d = x.shape
    dff = wg.shape[1]
    assert t % tm == 0 and dff % tf == 0 and dff // tf >= 2
    return pl.pallas_call(
        functools.partial(_ffn_kernel, final_norm=final_norm),
        grid=(t // tm, dff // tf),
        in_specs=[
            pl.BlockSpec(memory_space=pl.ANY),
            pl.BlockSpec((1, d), lambda i, f: (0, 0)),
            pl.BlockSpec((d, tf), lambda i, f: (0, f)),
            pl.BlockSpec((d, tf), lambda i, f: (0, f)),
            pl.BlockSpec((tf, d), lambda i, f: (f, 0)),
            pl.BlockSpec((1, d), lambda i, f: (0, 0)),
        ],
        out_specs=pl.BlockSpec((tm, d), lambda i, f: (i, 0)),
        out_shape=jax.ShapeDtypeStruct((t, d), F32),
        scratch_shapes=[pltpu.VMEM((tm, d), BF16), pltpu.VMEM((tm, d), F32),
                        pltpu.SemaphoreType.DMA(())],
        compiler_params=_params("arbitrary", "arbitrary"),
        name="ffn_final" if final_norm else "ffn",
    )(x, norm_g, wg, wu, wd, final_g)


def _in_proj_kernel(x_ref, g_ref, w_ref, qg_ref, kg_ref, cos_ref, sa_ref, sb_ref,
                    o_ref, xn_ref, *, tn, n_qk_tiles, sbq_lo, sbq_hi, scale):
    j = pl.program_id(1)

    tm = o_ref.shape[0]
    chunk = min(tm, ROW_CHUNK)

    def qk_tile(first):
        w = w_ref[:, pl.ds(pl.multiple_of(j * tn, tn), tn)]
        is_q = j < n_qk_tiles // 2
        gain = jnp.where(is_q, qg_ref[...], kg_ref[...])
        post = jnp.where(is_q, scale, 1.0).astype(F32)
        for r in range(tm // chunk):
            rows = slice(r * chunk, (r + 1) * chunk)
            if first:
                xn = _rms(x_ref[rows, :], g_ref[...]).astype(BF16)
                xn_ref[rows, :] = xn
            else:
                xn = xn_ref[rows, :]
            u = _dot(xn, w)
            cos, sa, sb = cos_ref[rows, :], sa_ref[rows, :], sb_ref[rows, :]
            for c in range(tn // QK_DIM):
                t = u[:, c * QK_DIM:(c + 1) * QK_DIM]
                ms = jnp.mean(t * t, axis=-1, keepdims=True)
                t = t * (lax.rsqrt(ms + EPS) * post) * gain
                t = (t * cos + pltpu.roll(t, QK_DIM - ROPE_DIM // 2, 1) * sa
                     + pltpu.roll(t, ROPE_DIM // 2, 1) * sb)
                o_ref[rows, c * QK_DIM:(c + 1) * QK_DIM] = t.astype(BF16)

    @pl.when(j == 0)
    def _():
        qk_tile(True)

    @pl.when((j > 0) & (j < n_qk_tiles))
    def _():
        qk_tile(False)

    @pl.when(j >= n_qk_tiles)
    def _():
        post = jnp.where((j >= sbq_lo) & (j < sbq_hi), scale, 1.0).astype(F32)
        u = _dot(xn_ref[...], w_ref[:, pl.ds(pl.multiple_of(j * tn, tn), tn)])
        o_ref[...] = (u * post).astype(BF16)


def _in_proj(x, norm_g, w, qg, kg, cos_t, sa_t, sb_t, *, seq, tm, tn):
    t, d = x.shape
    n = w.shape[1]
    diff_qk_cols = DIFF_HEADS * QK_DIM
    assert t % tm == 0 and seq % tm == 0 and n % tn == 0 and (2 * diff_qk_cols) % tn == 0
    assert (DIFF_HEADS * DIFF_V_DIM) % tn == 0 and (SB_HEADS * QK_DIM) % tn == 0
    n_qk_tiles = 4 * diff_qk_cols // tn
    sbq_lo = (4 * diff_qk_cols + DIFF_HEADS * DIFF_V_DIM) // tn
    sbq_hi = sbq_lo + SB_HEADS * QK_DIM // tn
    s_blocks = seq // tm
    rope_spec = pl.BlockSpec((tm, QK_DIM), lambda i, j: (i % s_blocks, 0))
    kern = functools.partial(_in_proj_kernel, tn=tn, n_qk_tiles=n_qk_tiles,
                             sbq_lo=sbq_lo, sbq_hi=sbq_hi, scale=QK_DIM ** -0.5 * LOG2E)
    return pl.pallas_call(
        kern,
        grid=(t // tm, n // tn),
        in_specs=[
            pl.BlockSpec((tm, d), lambda i, j: (i, 0)),
            pl.BlockSpec((1, d), lambda i, j: (0, 0)),
            pl.BlockSpec((d, n), lambda i, j: (0, 0), pipeline_mode=pl.Buffered(1)),
            pl.BlockSpec((1, QK_DIM), lambda i, j: (0, 0)),
            pl.BlockSpec((1, QK_DIM), lambda i, j: (0, 0)),
            rope_spec, rope_spec, rope_spec,
        ],
        out_specs=pl.BlockSpec((tm, tn), lambda i, j: (i, j)),
        out_shape=jax.ShapeDtypeStruct((t, n), BF16),
        scratch_shapes=[pltpu.VMEM((tm, d), BF16)],
        compiler_params=_params("parallel", "arbitrary"),
        name="in_proj",
    )(x, norm_g, w, qg, kg, cos_t, sa_t, sb_t)


def _diff_attn_kernel(q1_ref, q2_ref, k1_ref, k2_ref, v_ref, lq1_ref, lk1_ref, lq2_ref,
                      lk2_ref, sub_ref, o_ref, vt_ref, sa_ref, sb_ref, m_ref, l_ref, acc_ref,
                      *, tk, heads):
    qi = pl.program_id(2)
    tq = 2 * tk

    @pl.when(qi == 0)
    def _():
        vt_ref[...] = v_ref[...].astype(F32).T.astype(BF16)

    q_refs = (q1_ref, q2_ref)
    k_refs = (k1_ref, k2_ref)
    krow = lax.broadcasted_iota(jnp.int32, (tk, tq), 0)
    kcol = lax.broadcasted_iota(jnp.int32, (tk, tq), 1)

    chains = [(g, mp) for g in range(heads) for mp in range(2)]

    every_query = slice(0, tq)
    late_queries = slice(tk, tq)

    def scores(j, s_ref, qs=every_query):
        start = pl.multiple_of(j * tk, tk)
        width = qs.stop - qs.start
        for idx, (g, mp) in enumerate(chains):
            hd = slice(g * QK_DIM, (g + 1) * QK_DIM)
            s_ref[idx, :, :width] = _dot_nt(k_refs[mp][pl.ds(start, tk), hd], q_refs[mp][qs, hd])

    def consume(j, s_ref, visible=None, qs=every_query):
        start = pl.multiple_of(j * tk, tk)
        width = qs.stop - qs.start
        probs, alphas = [], []
        for idx in range(len(chains)):
            s = s_ref[idx, :, :width]
            if visible is not None:
                s = jnp.where(visible, s, -jnp.inf)
            m_old = m_ref[idx, :, qs]
            m = jnp.maximum(m_old, jnp.max(s, axis=0, keepdims=True))
            alpha = jnp.exp2(m_old - m)
            p = jnp.exp2(s - m)
            l_ref[idx, :, qs] = alpha * l_ref[idx, :, qs] + jnp.sum(p, axis=0, keepdims=True)
            m_ref[idx, :, qs] = m
            alphas.append(alpha)
            probs.append(p.astype(BF16))
        for idx, (g, mp) in enumerate(chains):
            vt = vt_ref[g * DIFF_V_DIM:(g + 1) * DIFF_V_DIM, pl.ds(start, tk)]
            acc_ref[idx, :, qs] = alphas[idx] * acc_ref[idx, :, qs] + _dot(vt, probs[idx])

    m_ref[...] = jnp.full(m_ref.shape, -jnp.inf, F32)
    l_ref[...] = jnp.zeros(l_ref.shape, F32)
    acc_ref[...] = jnp.zeros(acc_ref.shape, F32)

    scores(0, sa_ref)

    def body(p, carry):
        j = 2 * p
        scores(j + 1, sb_ref)
        consume(j, sa_ref)
        scores(j + 2, sa_ref)
        consume(j + 1, sb_ref)
        return carry

    lax.fori_loop(0, qi, body, 0)
    triangle = (lax.broadcasted_iota(jnp.int32, (tk, tk), 0)
                <= lax.broadcasted_iota(jnp.int32, (tk, tk), 1))
    scores(2 * qi + 1, sb_ref, late_queries)
    consume(2 * qi, sa_ref, krow <= kcol)
    consume(2 * qi + 1, sb_ref, triangle, late_queries)

    lam = (jnp.exp(jnp.sum(lq1_ref[...] * lk1_ref[...], axis=-1, keepdims=True))
           - jnp.exp(jnp.sum(lq2_ref[...] * lk2_ref[...], axis=-1, keepdims=True))
           + LAMBDA_INIT)
    for g in range(heads):
        o = (acc_ref[2 * g] * (1.0 / l_ref[2 * g])
             - acc_ref[2 * g + 1] * (lam / l_ref[2 * g + 1]))
        ms = jnp.mean(o * o, axis=0, keepdims=True)
        o = (o * lax.rsqrt(ms + EPS)).T * sub_ref[...] * (1.0 - LAMBDA_INIT)
        o_ref[:, g * DIFF_V_DIM:(g + 1) * DIFF_V_DIM] = o.astype(o_ref.dtype)


def _diff_attn(u, lq1, lk1, lq2, lk2, subln, *, batch, seq, tk, heads):
    t = u.shape[0]
    blk = 2 * tk
    nq = seq // blk
    ng = DIFF_HEADS // heads
    assert DIFF_HEADS % heads == 0 and seq % blk == 0
    v_off = 4 * DIFF_HEADS * QK_DIM // (heads * DIFF_V_DIM)
    lam_spec = pl.BlockSpec((1, QK_DIM), lambda b, h, q: (0, 0))
    return pl.pallas_call(
        functools.partial(_diff_attn_kernel, tk=tk, heads=heads),
        grid=(batch, ng, nq),
        in_specs=[
            pl.BlockSpec((blk, heads * QK_DIM), lambda b, h, q: (b * nq + q, h)),
            pl.BlockSpec((blk, heads * QK_DIM), lambda b, h, q: (b * nq + q, ng + h)),
            pl.BlockSpec((seq, heads * QK_DIM), lambda b, h, q: (b, 2 * ng + h)),
            pl.BlockSpec((seq, heads * QK_DIM), lambda b, h, q: (b, 3 * ng + h)),
            pl.BlockSpec((seq, heads * DIFF_V_DIM), lambda b, h, q: (b, v_off + h)),
            lam_spec, lam_spec, lam_spec, lam_spec,
            pl.BlockSpec((1, DIFF_V_DIM), lambda b, h, q: (0, 0)),
        ],
        out_specs=pl.BlockSpec((blk, heads * DIFF_V_DIM), lambda b, h, q: (b * nq + q, h)),
        out_shape=jax.ShapeDtypeStruct((t, DIFF_HEADS * DIFF_V_DIM), BF16),
        scratch_shapes=[
            pltpu.VMEM((heads * DIFF_V_DIM, seq), BF16),
            pltpu.VMEM((2 * heads, tk, blk), F32),
            pltpu.VMEM((2 * heads, tk, blk), F32),
            pltpu.VMEM((2 * heads, 1, blk), F32),
            pltpu.VMEM((2 * heads, 1, blk), F32),
            pltpu.VMEM((2 * heads, DIFF_V_DIM, blk), F32),
        ],
        compiler_params=_params("parallel", "parallel", "arbitrary"),
        name="diff_attn",
    )(u, u, u, u, u, lq1, lk1, lq2, lk2, subln)


def _sb_attn_kernel(q_ref, k_ref, v_ref, o_ref, vt_ref, up_ref, sa_ref, sb_ref, c_ref, acc_ref,
                    *, tk, heads):
    qi = pl.program_id(2)
    tq = 2 * tk
    krow = lax.broadcasted_iota(jnp.int32, (tk, tq), 0)
    kcol = lax.broadcasted_iota(jnp.int32, (tk, tq), 1)

    @pl.when(qi == 0)
    def _():
        vt_ref[...] = v_ref[...].astype(F32).T.astype(BF16)
        r = lax.broadcasted_iota(jnp.int32, (tk, tk), 0)
        c = lax.broadcasted_iota(jnp.int32, (tk, tk), 1)
        upper = (c >= r).astype(BF16)
        upper = jnp.concatenate([upper, upper], axis=1)
        up_ref[...] = jnp.concatenate([upper, jnp.ones((SUM_ROWS, 2 * tk), BF16)], axis=0)

    heads_hd = [slice(g * QK_DIM, (g + 1) * QK_DIM) for g in range(heads)]
    every_query = slice(0, tq)
    late_queries = slice(tk, tq)

    def scores(j, s_ref, qs=every_query):
        start = pl.multiple_of(j * tk, tk)
        width = qs.stop - qs.start
        for g, hd in enumerate(heads_hd):
            s_ref[g, :, :width] = _dot_nt(k_ref[pl.ds(start, tk), hd], q_ref[qs, hd])

    def consume(j, s_ref, visible=None, qs=every_query):
        start = pl.multiple_of(j * tk, tk)
        width = qs.stop - qs.start
        logits, hilo = [], []
        for g in range(heads):
            z = s_ref[g, :, :width]
            if visible is not None:
                z = jnp.where(visible, z, -MASKED_LOGIT)
            sp = jnp.maximum(z, 0.0) + jnp.log2(1.0 + jnp.exp2(-jnp.abs(z)))
            logits.append(z - c_ref[g, 0:1, qs])
            hi = sp.astype(BF16)
            lo = (sp - hi.astype(F32)).astype(BF16)
            hilo.append(jnp.concatenate([hi, lo], axis=0))
        sums = [_dot(up_ref[...], x) for x in hilo]
        weights = []
        for g in range(heads):
            weights.append(jnp.exp2(logits[g] - sums[g][:tk]).astype(BF16))
            c_ref[g, :, qs] += sums[g][tk:tk + C_ROWS]
        for g, hd in enumerate(heads_hd):
            acc_ref[g, :, qs] += _dot(vt_ref[hd, pl.ds(start, tk)], weights[g])

    c_ref[...] = jnp.zeros(c_ref.shape, F32)
    acc_ref[...] = jnp.zeros(acc_ref.shape, F32)

    last = 2 * qi + 1

    def block(i):
        return jnp.maximum(last - i, 0)

    triangle = (lax.broadcasted_iota(jnp.int32, (tk, tk), 0)
                < lax.broadcasted_iota(jnp.int32, (tk, tk), 1))
    scores(last, sa_ref, late_queries)
    scores(block(1), sb_ref)
    consume(last, sa_ref, triangle, late_queries)
    scores(block(2), sa_ref)
    consume(block(1), sb_ref, krow < kcol)

    def weights_alive():
        return jnp.min(c_ref[:, 0:1, :]) < UNDERFLOW_LOG2

    def more(p):
        return (p < qi) & weights_alive()

    def body(p):
        i = 2 + 2 * p
        scores(block(i + 1), sb_ref)
        consume(block(i), sa_ref)

        @pl.when(weights_alive())
        def _():
            scores(block(i + 2), sa_ref)
            consume(block(i + 1), sb_ref)

        return p + 1

    lax.while_loop(more, body, 0)

    for g in range(heads):
        o_ref[:, g * QK_DIM:(g + 1) * QK_DIM] = acc_ref[g].T.astype(o_ref.dtype)


def _sb_attn(u, *, batch, seq, tk, heads):
    t = u.shape[0]
    blk = 2 * tk
    nq = seq // blk
    ng = SB_HEADS // heads
    assert SB_HEADS % heads == 0 and seq % blk == 0
    w = heads * QK_DIM
    off = (4 * DIFF_HEADS * QK_DIM + DIFF_HEADS * DIFF_V_DIM) // w
    return pl.pallas_call(
        functools.partial(_sb_attn_kernel, tk=tk, heads=heads),
        grid=(batch, ng, nq),
        in_specs=[
            pl.BlockSpec((blk, w), lambda b, h, q: (b * nq + q, off + h)),
            pl.BlockSpec((seq, w), lambda b, h, q: (b, off + ng + h)),
            pl.BlockSpec((seq, w), lambda b, h, q: (b, off + 2 * ng + h)),
        ],
        out_specs=pl.BlockSpec((blk, w), lambda b, h, q: (b * nq + q, h)),
        out_shape=jax.ShapeDtypeStruct((t, SB_HEADS * QK_DIM), BF16),
        scratch_shapes=[
            pltpu.VMEM((w, seq), BF16),
            pltpu.VMEM((tk + SUM_ROWS, 2 * tk), BF16),
            pltpu.VMEM((heads, tk, blk), F32),
            pltpu.VMEM((heads, tk, blk), F32),
            pltpu.VMEM((heads, C_ROWS, blk), F32),
            pltpu.VMEM((heads, QK_DIM, blk), F32),
        ],
        compiler_params=_params("parallel", "parallel", "arbitrary"),
        name="sb_attn",
    )(u, u, u)


def _out_proj_kernel(x_ref, a_ref, b_ref, wa_ref, wb_ref, o_ref):
    o_ref[...] = x_ref[...] + _dot(a_ref[...], wa_ref[...]) + _dot(b_ref[...], wb_ref[...])


def _out_proj(x, a, b, w, *, tm):
    t, d = x.shape
    wa_rows, wb_rows = a.shape[1], b.shape[1]
    assert t % tm == 0 and wa_rows == wb_rows and w.shape[0] == wa_rows + wb_rows
    return pl.pallas_call(
        _out_proj_kernel,
        grid=(t // tm,),
        in_specs=[
            pl.BlockSpec((tm, d), lambda i: (i, 0)),
            pl.BlockSpec((tm, wa_rows), lambda i: (i, 0)),
            pl.BlockSpec((tm, wb_rows), lambda i: (i, 0)),
            pl.BlockSpec((wa_rows, d), lambda i: (0, 0)),
            pl.BlockSpec((wb_rows, d), lambda i: (1, 0)),
        ],
        out_specs=pl.BlockSpec((tm, d), lambda i: (i, 0)),
        out_shape=jax.ShapeDtypeStruct((t, d), F32),
        compiler_params=_params("parallel"),
        name="out_proj",
    )(x, a, b, w, w)


def _rope_tables(seq):
    half = ROPE_DIM // 2
    pos = jnp.arange(seq, dtype=F32)
    inv_freq = ROPE_THETA ** (-jnp.arange(0, ROPE_DIM, 2, dtype=F32) / ROPE_DIM)
    ang = pos[:, None] * inv_freq[None, :]
    cos, sin = jnp.cos(ang), jnp.sin(ang)
    ones = jnp.ones((seq, QK_DIM - ROPE_DIM), F32)
    zeros_h = jnp.zeros((seq, half), F32)
    zeros_r = jnp.zeros((seq, QK_DIM - ROPE_DIM), F32)
    cos_t = jnp.concatenate([cos, cos, ones], axis=1)
    sa_t = jnp.concatenate([-sin, zeros_h, zeros_r], axis=1)
    sb_t = jnp.concatenate([zeros_h, sin, zeros_r], axis=1)
    return cos_t, sa_t, sb_t


def _pick(n, pref):
    for c in pref:
        if n % c == 0:
            return c
    return n


def kernel(x, ffn1_norm, ffn1_w_gate, ffn1_w_up, ffn1_w_down, mix_norm, w_in, q_norm, k_norm,
           lambda_q1, lambda_k1, lambda_q2, lambda_k2, subln, w_out,
           ffn2_norm, ffn2_w_gate, ffn2_w_up, ffn2_w_down, final_norm):
    batch, seq, d = x.shape
    assert ffn1_norm.shape[0] == 1, "single-layer block"
    t = batch * seq
    l = 0
    dff = ffn1_w_gate.shape[-1]
    tm = _pick(t, (1024, 512, 256, 128))
    tf = _pick(dff, (512, 256, 128))
    tk = _pick(seq // 2, (256, 128))
    bf = lambda w: w[l].astype(BF16)
    mat = lambda w: w.reshape(w.shape[1:])
    row = lambda g: g[l].astype(F32)[None, :]

    xt = x.reshape(t, d)
    x1 = _ffn(xt, row(ffn1_norm), mat(ffn1_w_gate), mat(ffn1_w_up), mat(ffn1_w_down),
              row(final_norm), final_norm=False, tm=tm, tf=tf)

    cos_t, sa_t, sb_t = _rope_tables(seq)
    u = _in_proj(x1, row(mix_norm), bf(w_in), row(q_norm), row(k_norm), cos_t, sa_t, sb_t,
                 seq=seq, tm=_pick(seq, (1024, 512, 256, 128)), tn=1024)

    a = _diff_attn(u, row(lambda_q1), row(lambda_k1), row(lambda_q2), row(lambda_k2),
                   row(subln), batch=batch, seq=seq, tk=tk, heads=4)
    b = _sb_attn(u, batch=batch, seq=seq, tk=tk, heads=4)

    x2 = _out_proj(x1, a, b, bf(w_out), tm=_pick(t, (512, 256, 128)))

    out = _ffn(x2, row(ffn2_norm), mat(ffn2_w_gate), mat(ffn2_w_up), mat(ffn2_w_down),
               row(final_norm), final_norm=True, tm=tm, tf=tf)
    return out.reshape(batch, seq, d)
```

```python
import functools
import math

import jax
import jax.numpy as jnp
from jax import lax
from jax.experimental import pallas as pl
from jax.experimental.pallas import tpu as pltpu

EPS = 1e-5
ROPE_THETA = 500000.0
LAMBDA_INIT = 0.8 - 0.6 * math.exp(-0.3 * 0)
LOG2E = math.log2(math.e)
MASKED_LOGIT = 1e30
UNDERFLOW_LOG2 = 160.0

DIFF_HEADS = 4
SB_HEADS = 8
QK_DIM = 128
DIFF_V_DIM = 2 * QK_DIM
ROPE_DIM = QK_DIM // 4

F32 = jnp.float32
BF16 = jnp.bfloat16

V7X_VMEM_BYTES = 64 * 1024 * 1024
VMEM_LIMIT_BYTES = V7X_VMEM_BYTES - 4 * 1024 * 1024
ROW_CHUNK = 256
SUM_ROWS = 16
C_ROWS = 8


def _params(*sem):
    return pltpu.CompilerParams(dimension_semantics=sem, vmem_limit_bytes=VMEM_LIMIT_BYTES)


def _rms(x, g):
    ms = jnp.mean(x * x, axis=-1, keepdims=True)
    return x * lax.rsqrt(ms + EPS) * g


def _dot(a, b):
    return jnp.dot(a, b, preferred_element_type=F32)


def _dot_nt(a, b):
    return lax.dot_general(a, b, (((1,), (1,)), ((), ())), preferred_element_type=F32)


def _ffn_kernel(x_hbm, g_ref, wg_ref, wu_ref, wd_ref, fg_ref, o_ref, xn_ref, x_ref, x_sem,
                *, final_norm):
    i, f = pl.program_id(0), pl.program_id(1)
    tm = x_ref.shape[0]

    def x_copy(tile):
        return pltpu.make_async_copy(x_hbm.at[pl.ds(tile * tm, tm), :], x_ref, x_sem)

    @pl.when((i == 0) & (f == 0))
    def _():
        x_copy(0).start()

    def hidden_tile(xn, base):
        gate = _dot(xn, wg_ref[...].astype(BF16))
        up = _dot(xn, wu_ref[...].astype(BF16))
        h = (gate * jax.nn.sigmoid(gate)) * (up * 0.5)
        o_ref[...] = base + _dot(h.astype(BF16), wd_ref[...].astype(BF16))

    @pl.when(f == 0)
    def _():
        x_copy(i).wait()
        x = x_ref[...]
        xn = _rms(x, g_ref[...]).astype(BF16)
        xn_ref[...] = xn
        hidden_tile(xn, x)

    @pl.when((f == 1) & (i + 1 < pl.num_programs(0)))
    def _():
        x_copy(i + 1).start()

    @pl.when(f > 0)
    def _():
        hidden_tile(xn_ref[...], o_ref[...])

    if final_norm:
        @pl.when(f == pl.num_programs(1) - 1)
        def _():
            o_ref[...] = _rms(o_ref[...], fg_ref[...])


def _ffn(x, norm_g, wg, wu, wd, final_g, *, final_norm, tm, tf):
    t, d = x.shape
    dff = wg.shape[1]
    assert t % tm == 0 and dff % tf == 0 and dff // tf >= 2
    return pl.pallas_call(
        functools.partial(_ffn_kernel, final_norm=final_norm),
        grid=(t // tm, dff // tf),
        in_specs=[
            pl.BlockSpec(memory_space=pl.ANY),
            pl.BlockSpec((1, d), lambda i, f: (0, 0)),
            pl.BlockSpec((d, tf), lambda i, f: (0, f)),
            pl.BlockSpec((d, tf), lambda i, f: (0, f)),
            pl.BlockSpec((tf, d), lambda i, f: (f, 0)),
            pl.BlockSpec((1, d), lambda i, f: (0, 0)),
        ],
        out_specs=pl.BlockSpec((tm, d), lambda i, f: (i, 0)),
        out_shape=jax.ShapeDtypeStruct((t, d), F32),
        scratch_shapes=[pltpu.VMEM((tm, d), BF16), pltpu.VMEM((tm, d), F32),
                        pltpu.SemaphoreType.DMA(())],
        compiler_params=_params("arbitrary", "arbitrary"),
        name="ffn_final" if final_norm else "ffn",
    )(x, norm_g, wg, wu, wd, final_g)


def _in_proj_kernel(x_ref, g_ref, w_ref, qg_ref, kg_ref, cos_ref, sa_ref, sb_ref,
                    o_ref, xn_ref, *, tn, n_qk_tiles, sbq_lo, sbq_hi, scale):
    j = pl.program_id(1)

    tm = o_ref.shape[0]
    chunk = min(tm, ROW_CHUNK)

    def qk_tile(first):
        w = w_ref[:, pl.ds(pl.multiple_of(j * tn, tn), tn)]
        is_q = j < n_qk_tiles // 2
        gain = jnp.where(is_q, qg_ref[...], kg_ref[...])
        post = jnp.where(is_q, scale, 1.0).astype(F32)
        for r in range(tm // chunk):
            rows = slice(r * chunk, (r + 1) * chunk)
            if first:
                xn = _rms(x_ref[rows, :], g_ref[...]).astype(BF16)
                xn_ref[rows, :] = xn
            else:
                xn = xn_ref[rows, :]
            u = _dot(xn, w)
            cos, sa, sb = cos_ref[rows, :], sa_ref[rows, :], sb_ref[rows, :]
            for c in range(tn // QK_DIM):
                t = u[:, c * QK_DIM:(c + 1) * QK_DIM]
                ms = jnp.mean(t * t, axis=-1, keepdims=True)
                t = t * (lax.rsqrt(ms + EPS) * post) * gain
                t = (t * cos + pltpu.roll(t, QK_DIM - ROPE_DIM // 2, 1) * sa
                     + pltpu.roll(t, ROPE_DIM // 2, 1) * sb)
                o_ref[rows, c * QK_DIM:(c + 1) * QK_DIM] = t.astype(BF16)

    @pl.when(j == 0)
    def _():
        qk_tile(True)

    @pl.when((j > 0) & (j < n_qk_tiles))
    def _():
        qk_tile(False)

    @pl.when(j >= n_qk_tiles)
    def _():
        post = jnp.where((j >= sbq_lo) & (j < sbq_hi), scale, 1.0).astype(F32)
        u = _dot(xn_ref[...], w_ref[:, pl.ds(pl.multiple_of(j * tn, tn), tn)])
        o_ref[...] = (u * post).astype(BF16)


def _in_proj(x, norm_g, w, qg, kg, cos_t, sa_t, sb_t, *, seq, tm, tn):
    t, d = x.shape
    n = w.shape[1]
    diff_qk_cols = DIFF_HEADS * QK_DIM
    assert t % tm == 0 and seq % tm == 0 and n % tn == 0 and (2 * diff_qk_cols) % tn == 0
    assert (DIFF_HEADS * DIFF_V_DIM) % tn == 0 and (SB_HEADS * QK_DIM) % tn == 0
    n_qk_tiles = 4 * diff_qk_cols // tn
    sbq_lo = (4 * diff_qk_cols + DIFF_HEADS * DIFF_V_DIM) // tn
    sbq_hi = sbq_lo + SB_HEADS * QK_DIM // tn
    s_blocks = seq // tm
    rope_spec = pl.BlockSpec((tm, QK_DIM), lambda i, j: (i % s_blocks, 0))
    kern = functools.partial(_in_proj_kernel, tn=tn, n_qk_tiles=n_qk_tiles,
                             sbq_lo=sbq_lo, sbq_hi=sbq_hi, scale=QK_DIM ** -0.5 * LOG2E)
    return pl.pallas_call(
        kern,
        grid=(t // tm, n // tn),
        in_specs=[
            pl.BlockSpec((tm, d), lambda i, j: (i, 0)),
            pl.BlockSpec((1, d), lambda i, j: (0, 0)),
            pl.BlockSpec((d, n), lambda i, j: (0, 0), pipeline_mode=pl.Buffered(1)),
            pl.BlockSpec((1, QK_DIM), lambda i, j: (0, 0)),
            pl.BlockSpec((1, QK_DIM), lambda i, j: (0, 0)),
            rope_spec, rope_spec, rope_spec,
        ],
        out_specs=pl.BlockSpec((tm, tn), lambda i, j: (i, j)),
        out_shape=jax.ShapeDtypeStruct((t, n), BF16),
        scratch_shapes=[pltpu.VMEM((tm, d), BF16)],
        compiler_params=_params("parallel", "arbitrary"),
        name="in_proj",
    )(x, norm_g, w, qg, kg, cos_t, sa_t, sb_t)


def _diff_attn_kernel(q1_ref, q2_ref, k1_ref, k2_ref, v_ref, lq1_ref, lk1_ref, lq2_ref,
                      lk2_ref, sub_ref, o_ref, vt_ref, sa_ref, sb_ref, m_ref, l_ref, acc_ref,
                      *, tk, heads):
    qi = pl.program_id(2)
    tq = 2 * tk

    @pl.when(qi == 0)
    def _():
        vt_ref[...] = v_ref[...].astype(F32).T.astype(BF16)

    q_refs = (q1_ref, q2_ref)
    k_refs = (k1_ref, k2_ref)
    krow = lax.broadcasted_iota(jnp.int32, (tk, tq), 0)
    kcol = lax.broadcasted_iota(jnp.int32, (tk, tq), 1)

    chains = [(g, mp) for g in range(heads) for mp in range(2)]

    every_query = slice(0, tq)
    late_queries = slice(tk, tq)

    def scores(j, s_ref, qs=every_query):
        start = pl.multiple_of(j * tk, tk)
        width = qs.stop - qs.start
        for idx, (g, mp) in enumerate(chains):
            hd = slice(g * QK_DIM, (g + 1) * QK_DIM)
            s_ref[idx, :, :width] = _dot_nt(k_refs[mp][pl.ds(start, tk), hd], q_refs[mp][qs, hd])

    def consume(j, s_ref, visible=None, qs=every_query):
        start = pl.multiple_of(j * tk, tk)
        width = qs.stop - qs.start
        probs, alphas = [], []
        for idx in range(len(chains)):
            s = s_ref[idx, :, :width]
            if visible is not None:
                s = jnp.where(visible, s, -jnp.inf)
            m_old = m_ref[idx, :, qs]
            m = jnp.maximum(m_old, jnp.max(s, axis=0, keepdims=True))
            alpha = jnp.exp2(m_old - m)
            p = jnp.exp2(s - m)
            l_ref[idx, :, qs] = alpha * l_ref[idx, :, qs] + jnp.sum(p, axis=0, keepdims=True)
            m_ref[idx, :, qs] = m
            alphas.append(alpha)
            probs.append(p.astype(BF16))
        for idx, (g, mp) in enumerate(chains):
            vt = vt_ref[g * DIFF_V_DIM:(g + 1) * DIFF_V_DIM, pl.ds(start, tk)]
            acc_ref[idx, :, qs] = alphas[idx] * acc_ref[idx, :, qs] + _dot(vt, probs[idx])

    m_ref[...] = jnp.full(m_ref.shape, -jnp.inf, F32)
    l_ref[...] = jnp.zeros(l_ref.shape, F32)
    acc_ref[...] = jnp.zeros(acc_ref.shape, F32)

    scores(0, sa_ref)

    def body(p, carry):
        j = 2 * p
        scores(j + 1, sb_ref)
        consume(j, sa_ref)
        scores(j + 2, sa_ref)
        consume(j + 1, sb_ref)
        return carry

    lax.fori_loop(0, qi, body, 0)
    triangle = (lax.broadcasted_iota(jnp.int32, (tk, tk), 0)
                <= lax.broadcasted_iota(jnp.int32, (tk, tk), 1))
    scores(2 * qi + 1, sb_ref, late_queries)
    consume(2 * qi, sa_ref, krow <= kcol)
    consume(2 * qi + 1, sb_ref, triangle, late_queries)

    lam = (jnp.exp(jnp.sum(lq1_ref[...] * lk1_ref[...], axis=-1, keepdims=True))
           - jnp.exp(jnp.sum(lq2_ref[...] * lk2_ref[...], axis=-1, keepdims=True))
           + LAMBDA_INIT)
    for g in range(heads):
        o = (acc_ref[2 * g] * (1.0 / l_ref[2 * g])
             - acc_ref[2 * g + 1] * (lam / l_ref[2 * g + 1]))
        ms = jnp.mean(o * o, axis=0, keepdims=True)
        o = (o * lax.rsqrt(ms + EPS)).T * sub_ref[...] * (1.0 - LAMBDA_INIT)
        o_ref[:, g * DIFF_V_DIM:(g + 1) * DIFF_V_DIM] = o.astype(o_ref.dtype)


def _diff_attn(u, lq1, lk1, lq2, lk2, subln, *, batch, seq, tk, heads):
    t = u.shape[0]
    blk = 2 * tk
    nq = seq // blk
    ng = DIFF_HEADS // heads
    assert DIFF_HEADS % heads == 0 and seq % blk == 0
    v_off = 4 * DIFF_HEADS * QK_DIM // (heads * DIFF_V_DIM)
    lam_spec = pl.BlockSpec((1, QK_DIM), lambda b, h, q: (0, 0))
    return pl.pallas_call(
        functools.partial(_diff_attn_kernel, tk=tk, heads=heads),
        grid=(batch, ng, nq),
        in_specs=[
            pl.BlockSpec((blk, heads * QK_DIM), lambda b, h, q: (b * nq + q, h)),
            pl.BlockSpec((blk, heads * QK_DIM), lambda b, h, q: (b * nq + q, ng + h)),
            pl.BlockSpec((seq, heads * QK_DIM), lambda b, h, q: (b, 2 * ng + h)),
            pl.BlockSpec((seq, heads * QK_DIM), lambda b, h, q: (b, 3 * ng + h)),
            pl.BlockSpec((seq, heads * DIFF_V_DIM), lambda b, h, q: (b, v_off + h)),
            lam_spec, lam_spec, lam_spec, lam_spec,
            pl.BlockSpec((1, DIFF_V_DIM), lambda b, h, q: (0, 0)),
        ],
        out_specs=pl.BlockSpec((blk, heads * DIFF_V_DIM), lambda b, h, q: (b * nq + q, h)),
        out_shape=jax.ShapeDtypeStruct((t, DIFF_HEADS * DIFF_V_DIM), BF16),
        scratch_shapes=[
            pltpu.VMEM((heads * DIFF_V_DIM, seq), BF16),
            pltpu.VMEM((2 * heads, tk, blk), F32),
            pltpu.VMEM((2 * heads, tk, blk), F32),
            pltpu.VMEM((2 * heads, 1, blk), F32),
            pltpu.VMEM((2 * heads, 1, blk), F32),
            pltpu.VMEM((2 * heads, DIFF_V_DIM, blk), F32),
        ],
        compiler_params=_params("parallel", "parallel", "arbitrary"),
        name="diff_attn",
    )(u, u, u, u, u, lq1, lk1, lq2, lk2, subln)


def _sb_attn_kernel(q_ref, k_ref, v_ref, o_ref, vt_ref, up_ref, sa_ref, sb_ref, c_ref, acc_ref,
                    *, tk, heads):
    qi = pl.program_id(2)
    tq = 2 * tk
    krow = lax.broadcasted_iota(jnp.int32, (tk, tq), 0)
    kcol = lax.broadcasted_iota(jnp.int32, (tk, tq), 1)

    @pl.when(qi == 0)
    def _():
        vt_ref[...] = v_ref[...].astype(F32).T.astype(BF16)
        r = lax.broadcasted_iota(jnp.int32, (tk, tk), 0)
        c = lax.broadcasted_iota(jnp.int32, (tk, tk), 1)
        upper = (c >= r).astype(BF16)
        upper = jnp.concatenate([upper, upper], axis=1)
        up_ref[...] = jnp.concatenate([upper, jnp.ones((SUM_ROWS, 2 * tk), BF16)], axis=0)

    heads_hd = [slice(g * QK_DIM, (g + 1) * QK_DIM) for g in range(heads)]
    every_query = slice(0, tq)
    late_queries = slice(tk, tq)

    def scores(j, s_ref, qs=every_query):
        start = pl.multiple_of(j * tk, tk)
        width = qs.stop - qs.start
        for g, hd in enumerate(heads_hd):
            s_ref[g, :, :width] = _dot_nt(k_ref[pl.ds(start, tk), hd], q_ref[qs, hd])

    def consume(j, s_ref, visible=None, qs=every_query):
        start = pl.multiple_of(j * tk, tk)
        width = qs.stop - qs.start
        logits, hilo = [], []
        for g in range(heads):
            z = s_ref[g, :, :width]
            if visible is not None:
                z = jnp.where(visible, z, -MASKED_LOGIT)
            sp = jnp.maximum(z, 0.0) + jnp.log2(1.0 + jnp.exp2(-jnp.abs(z)))
            logits.append(z - c_ref[g, 0:1, qs])
            hi = sp.astype(BF16)
            lo = (sp - hi.astype(F32)).astype(BF16)
            hilo.append(jnp.concatenate([hi, lo], axis=0))
        sums = [_dot(up_ref[...], x) for x in hilo]
        weights = []
        for g in range(heads):
            weights.append(jnp.exp2(logits[g] - sums[g][:tk]).astype(BF16))
            c_ref[g, :, qs] += sums[g][tk:tk + C_ROWS]
        for g, hd in enumerate(heads_hd):
            acc_ref[g, :, qs] += _dot(vt_ref[hd, pl.ds(start, tk)], weights[g])

    c_ref[...] = jnp.zeros(c_ref.shape, F32)
    acc_ref[...] = jnp.zeros(acc_ref.shape, F32)

    last = 2 * qi + 1

    def block(i):
        return jnp.maximum(last - i, 0)

    triangle = (lax.broadcasted_iota(jnp.int32, (tk, tk), 0)
                < lax.broadcasted_iota(jnp.int32, (tk, tk), 1))
    scores(last, sa_ref, late_queries)
    scores(block(1), sb_ref)
    consume(last, sa_ref, triangle, late_queries)
    scores(block(2), sa_ref)
    consume(block(1), sb_ref, krow < kcol)

    def weights_alive():
        return jnp.min(c_ref[:, 0:1, :]) < UNDERFLOW_LOG2

    def more(p):
        return (p < qi) & weights_alive()

    def body(p):
        i = 2 + 2 * p
        scores(block(i + 1), sb_ref)
        consume(block(i), sa_ref)

        @pl.when(weights_alive())
        def _():
            scores(block(i + 2), sa_ref)
            consume(block(i + 1), sb_ref)

        return p + 1

    lax.while_loop(more, body, 0)

    for g in range(heads):
        o_ref[:, g * QK_DIM:(g + 1) * QK_DIM] = acc_ref[g].T.astype(o_ref.dtype)


def _sb_attn(u, *, batch, seq, tk, heads):
    t = u.shape[0]
    blk = 2 * tk
    nq = seq // blk
    ng = SB_HEADS // heads
    assert SB_HEADS % heads == 0 and seq % blk == 0
    w = heads * QK_DIM
    off = (4 * DIFF_HEADS * QK_DIM + DIFF_HEADS * DIFF_V_DIM) // w
    return pl.pallas_call(
        functools.partial(_sb_attn_kernel, tk=tk, heads=heads),
        grid=(batch, ng, nq),
        in_specs=[
            pl.BlockSpec((blk, w), lambda b, h, q: (b * nq + q, off + h)),
            pl.BlockSpec((seq, w), lambda b, h, q: (b, off + ng + h)),
            pl.BlockSpec((seq, w), lambda b, h, q: (b, off + 2 * ng + h)),
        ],
        out_specs=pl.BlockSpec((blk, w), lambda b, h, q: (b * nq + q, h)),
        out_shape=jax.ShapeDtypeStruct((t, SB_HEADS * QK_DIM), BF16),
        scratch_shapes=[
            pltpu.VMEM((w, seq), BF16),
            pltpu.VMEM((tk + SUM_ROWS, 2 * tk), BF16),
            pltpu.VMEM((heads, tk, blk), F32),
            pltpu.VMEM((heads, tk, blk), F32),
            pltpu.VMEM((heads, C_ROWS, blk), F32),
            pltpu.VMEM((heads, QK_DIM, blk), F32),
        ],
        compiler_params=_params("parallel", "parallel", "arbitrary"),
        name="sb_attn",
    )(u, u, u)


def _out_proj_kernel(x_ref, a_ref, b_ref, wa_ref, wb_ref, o_ref):
    o_ref[...] = x_ref[...] + _dot(a_ref[...], wa_ref[...]) + _dot(b_ref[...], wb_ref[...])


def _out_proj(x, a, b, w, *, tm):
    t, d = x.shape
    wa_rows, wb_rows = a.shape[1], b.shape[1]
    assert t % tm == 0 and wa_rows == wb_rows and w.shape[0] == wa_rows + wb_rows
    return pl.pallas_call(
        _out_proj_kernel,
        grid=(t // tm,),
        in_specs=[
            pl.BlockSpec((tm, d), lambda i: (i, 0)),
            pl.BlockSpec((tm, wa_rows), lambda i: (i, 0)),
            pl.BlockSpec((tm, wb_rows), lambda i: (i, 0)),
            pl.BlockSpec((wa_rows, d), lambda i: (0, 0), pipeline_mode=pl.Buffered(1)),
            pl.BlockSpec((wb_rows, d), lambda i: (1, 0), pipeline_mode=pl.Buffered(1)),
        ],
        out_specs=pl.BlockSpec((tm, d), lambda i: (i, 0)),
        out_shape=jax.ShapeDtypeStruct((t, d), F32),
        compiler_params=_params("parallel"),
        name="out_proj",
    )(x, a, b, w, w)


def _rope_tables(seq):
    half = ROPE_DIM // 2
    pos = jnp.arange(seq, dtype=F32)
    inv_freq = ROPE_THETA ** (-jnp.arange(0, ROPE_DIM, 2, dtype=F32) / ROPE_DIM)
    ang = pos[:, None] * inv_freq[None, :]
    cos, sin = jnp.cos(ang), jnp.sin(ang)
    ones = jnp.ones((seq, QK_DIM - ROPE_DIM), F32)
    zeros_h = jnp.zeros((seq, half), F32)
    zeros_r = jnp.zeros((seq, QK_DIM - ROPE_DIM), F32)
    cos_t = jnp.concatenate([cos, cos, ones], axis=1)
    sa_t = jnp.concatenate([-sin, zeros_h, zeros_r], axis=1)
    sb_t = jnp.concatenate([zeros_h, sin, zeros_r], axis=1)
    return cos_t, sa_t, sb_t


def _pick(n, pref):
    for c in pref:
        if n % c == 0:
            return c
    return n


def kernel(x, ffn1_norm, ffn1_w_gate, ffn1_w_up, ffn1_w_down, mix_norm, w_in, q_norm, k_norm,
           lambda_q1, lambda_k1, lambda_q2, lambda_k2, subln, w_out,
           ffn2_norm, ffn2_w_gate, ffn2_w_up, ffn2_w_down, final_norm):
    batch, seq, d = x.shape
    assert ffn1_norm.shape[0] == 1, "single-layer block"
    t = batch * seq
    l = 0
    dff = ffn1_w_gate.shape[-1]
    tm = _pick(t, (1024, 512, 256, 128))
    tf = _pick(dff, (512, 256, 128))
    tk = _pick(seq // 2, (256, 128))
    bf = lambda w: w[l].astype(BF16)
    mat = lambda w: w.reshape(w.shape[1:])
    row = lambda g: g[l].astype(F32)[None, :]

    xt = x.reshape(t, d)
    x1 = _ffn(xt, row(ffn1_norm), mat(ffn1_w_gate), mat(ffn1_w_up), mat(ffn1_w_down),
              row(final_norm), final_norm=False, tm=tm, tf=tf)

    cos_t, sa_t, sb_t = _rope_tables(seq)
    u = _in_proj(x1, row(mix_norm), bf(w_in), row(q_norm), row(k_norm), cos_t, sa_t, sb_t,
                 seq=seq, tm=_pick(seq, (1024, 512, 256, 128)), tn=1024)

    a = _diff_attn(u, row(lambda_q1), row(lambda_k1), row(lambda_q2), row(lambda_k2),
                   row(subln), batch=batch, seq=seq, tk=tk, heads=4)
    b = _sb_attn(u, batch=batch, seq=seq, tk=tk, heads=8)

    x2 = _out_proj(x1, a, b, bf(w_out), tm=tm)

    out = _ffn(x2, row(ffn2_norm), mat(ffn2_w_gate), mat(ffn2_w_up), mat(ffn2_w_down),
               row(final_norm), final_norm=True, tm=tm, tf=tf)
    return out.reshape(batch, seq, d)
```

```python
import functools
import math

import jax
import jax.numpy as jnp
from jax import lax
from jax.experimental import pallas as pl
from jax.experimental.pallas import tpu as pltpu

EPS = 1e-5
ROPE_THETA = 500000.0
LAMBDA_INIT = 0.8 - 0.6 * math.exp(-0.3 * 0)
LOG2E = math.log2(math.e)
MASKED_LOGIT = 1e30
UNDERFLOW_LOG2 = 160.0

DIFF_HEADS = 4
SB_HEADS = 8
QK_DIM = 128
DIFF_V_DIM = 2 * QK_DIM
ROPE_DIM = QK_DIM // 4

F32 = jnp.float32
BF16 = jnp.bfloat16

V7X_VMEM_BYTES = 64 * 1024 * 1024
VMEM_LIMIT_BYTES = V7X_VMEM_BYTES - 4 * 1024 * 1024
ROW_CHUNK = 256
SUM_ROWS = 16
C_ROWS = 8


def _params(*sem):
    return pltpu.CompilerParams(dimension_semantics=sem, vmem_limit_bytes=VMEM_LIMIT_BYTES)


def _rms(x, g):
    ms = jnp.mean(x * x, axis=-1, keepdims=True)
    return x * lax.rsqrt(ms + EPS) * g


def _dot(a, b):
    return jnp.dot(a, b, preferred_element_type=F32)


def _block_start(j, size):
    return j * size if isinstance(j, int) else pl.multiple_of(j * size, size)


def _dot_nt(a, b):
    return lax.dot_general(a, b, (((1,), (1,)), ((), ())), preferred_element_type=F32)


def _ffn_kernel(x_hbm, g_ref, wg_ref, wu_ref, wd_ref, fg_ref, o_ref, xn_ref, x_ref, x_sem,
                *, final_norm):
    i, f = pl.program_id(0), pl.program_id(1)
    tm = x_ref.shape[0]

    def x_copy(tile):
        return pltpu.make_async_copy(x_hbm.at[pl.ds(tile * tm, tm), :], x_ref, x_sem)

    @pl.when((i == 0) & (f == 0))
    def _():
        x_copy(0).start()

    def hidden_tile(xn, base):
        gate = _dot(xn, wg_ref[...].astype(BF16))
        up = _dot(xn, wu_ref[...].astype(BF16))
        h = (gate * jax.nn.sigmoid(gate)) * (up * 0.5)
        o_ref[...] = base + _dot(h.astype(BF16), wd_ref[...].astype(BF16))

    @pl.when(f == 0)
    def _():
        x_copy(i).wait()
        x = x_ref[...]
        xn = _rms(x, g_ref[...]).astype(BF16)
        xn_ref[...] = xn
        hidden_tile(xn, x)

    @pl.when((f == 1) & (i + 1 < pl.num_programs(0)))
    def _():
        x_copy(i + 1).start()

    @pl.when(f > 0)
    def _():
        hidden_tile(xn_ref[...], o_ref[...])

    if final_norm:
        @pl.when(f == pl.num_programs(1) - 1)
        def _():
            o_ref[...] = _rms(o_ref[...], fg_ref[...])


def _ffn(x, norm_g, wg, wu, wd, final_g, *, final_norm, tm, tf):
    t, d = x.shape
    dff = wg.shape[1]
    assert t % tm == 0 and dff % tf == 0 and dff // tf >= 2
    return pl.pallas_call(
        functools.partial(_ffn_kernel, final_norm=final_norm),
        grid=(t // tm, dff // tf),
        in_specs=[
            pl.BlockSpec(memory_space=pl.ANY),
            pl.BlockSpec((1, d), lambda i, f: (0, 0)),
            pl.BlockSpec((d, tf), lambda i, f: (0, f)),
            pl.BlockSpec((d, tf), lambda i, f: (0, f)),
            pl.BlockSpec((tf, d), lambda i, f: (f, 0)),
            pl.BlockSpec((1, d), lambda i, f: (0, 0)),
        ],
        out_specs=pl.BlockSpec((tm, d), lambda i, f: (i, 0)),
        out_shape=jax.ShapeDtypeStruct((t, d), F32),
        scratch_shapes=[pltpu.VMEM((tm, d), BF16), pltpu.VMEM((tm, d), F32),
                        pltpu.SemaphoreType.DMA(())],
        compiler_params=_params("arbitrary", "arbitrary"),
        name="ffn_final" if final_norm else "ffn",
    )(x, norm_g, wg, wu, wd, final_g)


def _in_proj_kernel(x_ref, g_ref, w_ref, qg_ref, kg_ref, cos_ref, sa_ref, sb_ref,
                    o_ref, xn_ref, *, tn, n_qk_tiles, sbq_lo, sbq_hi, scale):
    j = pl.program_id(1)

    tm = o_ref.shape[0]
    chunk = min(tm, ROW_CHUNK)

    def qk_tile(first):
        w = w_ref[:, pl.ds(pl.multiple_of(j * tn, tn), tn)]
        is_q = j < n_qk_tiles // 2
        gain = jnp.where(is_q, qg_ref[...], kg_ref[...])
        post = jnp.where(is_q, scale, 1.0).astype(F32)
        for r in range(tm // chunk):
            rows = slice(r * chunk, (r + 1) * chunk)
            if first:
                xn = _rms(x_ref[rows, :], g_ref[...]).astype(BF16)
                xn_ref[rows, :] = xn
            else:
                xn = xn_ref[rows, :]
            u = _dot(xn, w)
            cos, sa, sb = cos_ref[rows, :], sa_ref[rows, :], sb_ref[rows, :]
            for c in range(tn // QK_DIM):
                t = u[:, c * QK_DIM:(c + 1) * QK_DIM]
                ms = jnp.mean(t * t, axis=-1, keepdims=True)
                t = t * (lax.rsqrt(ms + EPS) * post) * gain
                t = (t * cos + pltpu.roll(t, QK_DIM - ROPE_DIM // 2, 1) * sa
                     + pltpu.roll(t, ROPE_DIM // 2, 1) * sb)
                o_ref[rows, c * QK_DIM:(c + 1) * QK_DIM] = t.astype(BF16)

    @pl.when(j == 0)
    def _():
        qk_tile(True)

    @pl.when((j > 0) & (j < n_qk_tiles))
    def _():
        qk_tile(False)

    @pl.when(j >= n_qk_tiles)
    def _():
        post = jnp.where((j >= sbq_lo) & (j < sbq_hi), scale, 1.0).astype(F32)
        u = _dot(xn_ref[...], w_ref[:, pl.ds(pl.multiple_of(j * tn, tn), tn)])
        o_ref[...] = (u * post).astype(BF16)


def _in_proj(x, norm_g, w, qg, kg, cos_t, sa_t, sb_t, *, seq, tm, tn):
    t, d = x.shape
    n = w.shape[1]
    diff_qk_cols = DIFF_HEADS * QK_DIM
    assert t % tm == 0 and seq % tm == 0 and n % tn == 0 and (2 * diff_qk_cols) % tn == 0
    assert (DIFF_HEADS * DIFF_V_DIM) % tn == 0 and (SB_HEADS * QK_DIM) % tn == 0
    n_qk_tiles = 4 * diff_qk_cols // tn
    sbq_lo = (4 * diff_qk_cols + DIFF_HEADS * DIFF_V_DIM) // tn
    sbq_hi = sbq_lo + SB_HEADS * QK_DIM // tn
    s_blocks = seq // tm
    rope_spec = pl.BlockSpec((tm, QK_DIM), lambda i, j: (i % s_blocks, 0))
    kern = functools.partial(_in_proj_kernel, tn=tn, n_qk_tiles=n_qk_tiles,
                             sbq_lo=sbq_lo, sbq_hi=sbq_hi, scale=QK_DIM ** -0.5 * LOG2E)
    return pl.pallas_call(
        kern,
        grid=(t // tm, n // tn),
        in_specs=[
            pl.BlockSpec((tm, d), lambda i, j: (i, 0)),
            pl.BlockSpec((1, d), lambda i, j: (0, 0)),
            pl.BlockSpec((d, n), lambda i, j: (0, 0), pipeline_mode=pl.Buffered(1)),
            pl.BlockSpec((1, QK_DIM), lambda i, j: (0, 0)),
            pl.BlockSpec((1, QK_DIM), lambda i, j: (0, 0)),
            rope_spec, rope_spec, rope_spec,
        ],
        out_specs=pl.BlockSpec((tm, tn), lambda i, j: (i, j)),
        out_shape=jax.ShapeDtypeStruct((t, n), BF16),
        scratch_shapes=[pltpu.VMEM((tm, d), BF16)],
        compiler_params=_params("parallel", "arbitrary"),
        name="in_proj",
    )(x, norm_g, w, qg, kg, cos_t, sa_t, sb_t)


def _diff_attn_kernel(q1_ref, q2_ref, k1_ref, k2_ref, v_ref, lq1_ref, lk1_ref, lq2_ref,
                      lk2_ref, sub_ref, o_ref, vt_ref, sa_ref, sb_ref, m_ref, l_ref, acc_ref,
                      *, tk, heads):
    tq = 2 * tk
    seq = q1_ref.shape[0]
    vt_ref[...] = v_ref[...].astype(F32).T.astype(BF16)

    q_refs = (q1_ref, q2_ref)
    k_refs = (k1_ref, k2_ref)
    krow = lax.broadcasted_iota(jnp.int32, (tk, tq), 0)
    kcol = lax.broadcasted_iota(jnp.int32, (tk, tq), 1)
    triangle = (lax.broadcasted_iota(jnp.int32, (tk, tk), 0)
                <= lax.broadcasted_iota(jnp.int32, (tk, tk), 1))
    lam = (jnp.exp(jnp.sum(lq1_ref[...] * lk1_ref[...], axis=-1, keepdims=True))
           - jnp.exp(jnp.sum(lq2_ref[...] * lk2_ref[...], axis=-1, keepdims=True))
           + LAMBDA_INIT)

    chains = [(g, mp) for g in range(heads) for mp in range(2)]

    every_query = slice(0, tq)
    late_queries = slice(tk, tq)

    for qi in range(seq // tq):
        _diff_query_block(qi, q_refs, k_refs, vt_ref, sub_ref, o_ref, sa_ref, sb_ref, m_ref,
                          l_ref, acc_ref, chains, krow, kcol, triangle, lam, every_query,
                          late_queries, tk=tk, heads=heads)


def _diff_query_block(qi, q_refs, k_refs, vt_ref, sub_ref, o_ref, sa_ref, sb_ref, m_ref, l_ref,
                      acc_ref, chains, krow, kcol, triangle, lam, every_query, late_queries,
                      *, tk, heads):
    tq = 2 * tk
    q0 = qi * tq

    def scores(j, s_ref, qs=every_query):
        start = _block_start(j, tk)
        width = qs.stop - qs.start
        rows = slice(q0 + qs.start, q0 + qs.stop)
        for idx, (g, mp) in enumerate(chains):
            hd = slice(g * QK_DIM, (g + 1) * QK_DIM)
            s_ref[idx, :, :width] = _dot_nt(k_refs[mp][pl.ds(start, tk), hd], q_refs[mp][rows, hd])

    def consume(j, s_ref, visible=None, qs=every_query):
        start = _block_start(j, tk)
        width = qs.stop - qs.start
        probs, alphas = [], []
        for idx in range(len(chains)):
            s = s_ref[idx, :, :width]
            if visible is not None:
                s = jnp.where(visible, s, -jnp.inf)
            m_old = m_ref[idx, :, qs]
            m = jnp.maximum(m_old, jnp.max(s, axis=0, keepdims=True))
            alpha = jnp.exp2(m_old - m)
            p = jnp.exp2(s - m)
            l_ref[idx, :, qs] = alpha * l_ref[idx, :, qs] + jnp.sum(p, axis=0, keepdims=True)
            m_ref[idx, :, qs] = m
            alphas.append(alpha)
            probs.append(p.astype(BF16))
        for idx, (g, mp) in enumerate(chains):
            vt = vt_ref[g * DIFF_V_DIM:(g + 1) * DIFF_V_DIM, pl.ds(start, tk)]
            acc_ref[idx, :, qs] = alphas[idx] * acc_ref[idx, :, qs] + _dot(vt, probs[idx])

    m_ref[...] = jnp.full(m_ref.shape, -jnp.inf, F32)
    l_ref[...] = jnp.zeros(l_ref.shape, F32)
    acc_ref[...] = jnp.zeros(acc_ref.shape, F32)

    scores(0, sa_ref)

    def body(p, carry):
        j = 2 * p
        scores(j + 1, sb_ref)
        consume(j, sa_ref)
        scores(j + 2, sa_ref)
        consume(j + 1, sb_ref)
        return carry

    if qi > 0:
        lax.fori_loop(0, qi, body, 0)
    scores(2 * qi + 1, sb_ref, late_queries)
    consume(2 * qi, sa_ref, krow <= kcol)
    consume(2 * qi + 1, sb_ref, triangle, late_queries)

    for g in range(heads):
        o = (acc_ref[2 * g] * (1.0 / l_ref[2 * g])
             - acc_ref[2 * g + 1] * (lam / l_ref[2 * g + 1]))
        ms = jnp.mean(o * o, axis=0, keepdims=True)
        o = (o * lax.rsqrt(ms + EPS)).T * sub_ref[...] * (1.0 - LAMBDA_INIT)
        o_ref[q0:q0 + tq, g * DIFF_V_DIM:(g + 1) * DIFF_V_DIM] = o.astype(o_ref.dtype)


def _diff_attn(u, lq1, lk1, lq2, lk2, subln, *, batch, seq, tk, heads):
    t = u.shape[0]
    blk = 2 * tk
    nq = seq // blk
    ng = DIFF_HEADS // heads
    assert DIFF_HEADS % heads == 0 and seq % blk == 0
    v_off = 4 * DIFF_HEADS * QK_DIM // (heads * DIFF_V_DIM)
    lam_spec = pl.BlockSpec((1, QK_DIM), lambda b, h: (0, 0))
    return pl.pallas_call(
        functools.partial(_diff_attn_kernel, tk=tk, heads=heads),
        grid=(batch, ng),
        in_specs=[
            pl.BlockSpec((seq, heads * QK_DIM), lambda b, h: (b, h)),
            pl.BlockSpec((seq, heads * QK_DIM), lambda b, h: (b, ng + h)),
            pl.BlockSpec((seq, heads * QK_DIM), lambda b, h: (b, 2 * ng + h)),
            pl.BlockSpec((seq, heads * QK_DIM), lambda b, h: (b, 3 * ng + h)),
            pl.BlockSpec((seq, heads * DIFF_V_DIM), lambda b, h: (b, v_off + h)),
            lam_spec, lam_spec, lam_spec, lam_spec,
            pl.BlockSpec((1, DIFF_V_DIM), lambda b, h: (0, 0)),
        ],
        out_specs=pl.BlockSpec((seq, heads * DIFF_V_DIM), lambda b, h: (b, h)),
        out_shape=jax.ShapeDtypeStruct((t, DIFF_HEADS * DIFF_V_DIM), BF16),
        scratch_shapes=[
            pltpu.VMEM((heads * DIFF_V_DIM, seq), BF16),
            pltpu.VMEM((2 * heads, tk, blk), F32),
            pltpu.VMEM((2 * heads, tk, blk), F32),
            pltpu.VMEM((2 * heads, 1, blk), F32),
            pltpu.VMEM((2 * heads, 1, blk), F32),
            pltpu.VMEM((2 * heads, DIFF_V_DIM, blk), F32),
        ],
        compiler_params=_params("parallel", "parallel"),
        name="diff_attn",
    )(u, u, u, u, u, lq1, lk1, lq2, lk2, subln)


def _sb_attn_kernel(q_ref, k_ref, v_ref, o_ref, vt_ref, up_ref, sa_ref, sb_ref, c_ref, acc_ref,
                    *, tk, heads):
    tq = 2 * tk
    seq = q_ref.shape[0]
    vt_ref[...] = v_ref[...].astype(F32).T.astype(BF16)
    r = lax.broadcasted_iota(jnp.int32, (tk, tk), 0)
    c = lax.broadcasted_iota(jnp.int32, (tk, tk), 1)
    upper = (c >= r).astype(BF16)
    upper = jnp.concatenate([upper, upper], axis=1)
    up_ref[...] = jnp.concatenate([upper, jnp.ones((SUM_ROWS, 2 * tk), BF16)], axis=0)

    for qi in range(seq // tq):
        _sb_query_block(qi, q_ref, k_ref, o_ref, vt_ref, up_ref, sa_ref, sb_ref, c_ref, acc_ref,
                        tk=tk, heads=heads)


def _sb_query_block(qi, q_ref, k_ref, o_ref, vt_ref, up_ref, sa_ref, sb_ref, c_ref, acc_ref,
                    *, tk, heads):
    tq = 2 * tk
    q0 = qi * tq
    krow = lax.broadcasted_iota(jnp.int32, (tk, tq), 0)
    kcol = lax.broadcasted_iota(jnp.int32, (tk, tq), 1)
    heads_hd = [slice(g * QK_DIM, (g + 1) * QK_DIM) for g in range(heads)]
    every_query = slice(0, tq)
    late_queries = slice(tk, tq)

    def scores(j, s_ref, qs=every_query):
        start = _block_start(j, tk)
        width = qs.stop - qs.start
        rows = slice(q0 + qs.start, q0 + qs.stop)
        for g, hd in enumerate(heads_hd):
            s_ref[g, :, :width] = _dot_nt(k_ref[pl.ds(start, tk), hd], q_ref[rows, hd])

    def consume(j, s_ref, visible=None, qs=every_query):
        start = _block_start(j, tk)
        width = qs.stop - qs.start
        logits, hilo = [], []
        for g in range(heads):
            z = s_ref[g, :, :width]
            if visible is not None:
                z = jnp.where(visible, z, -MASKED_LOGIT)
            sp = jnp.maximum(z, 0.0) + jnp.log2(1.0 + jnp.exp2(-jnp.abs(z)))
            logits.append(z - c_ref[g, 0:1, qs])
            hi = sp.astype(BF16)
            lo = (sp - hi.astype(F32)).astype(BF16)
            hilo.append(jnp.concatenate([hi, lo], axis=0))
        sums = [_dot(up_ref[...], x) for x in hilo]
        weights = []
        for g in range(heads):
            weights.append(jnp.exp2(logits[g] - sums[g][:tk]).astype(BF16))
            c_ref[g, :, qs] += sums[g][tk:tk + C_ROWS]
        for g, hd in enumerate(heads_hd):
            acc_ref[g, :, qs] += _dot(vt_ref[hd, pl.ds(start, tk)], weights[g])

    c_ref[...] = jnp.zeros(c_ref.shape, F32)
    acc_ref[...] = jnp.zeros(acc_ref.shape, F32)

    last = 2 * qi + 1

    def block(i):
        return max(last - i, 0) if isinstance(i, int) else jnp.maximum(last - i, 0)

    triangle = (lax.broadcasted_iota(jnp.int32, (tk, tk), 0)
                < lax.broadcasted_iota(jnp.int32, (tk, tk), 1))
    scores(last, sa_ref, late_queries)
    scores(block(1), sb_ref)
    consume(last, sa_ref, triangle, late_queries)
    scores(block(2), sa_ref)
    consume(block(1), sb_ref, krow < kcol)

    def weights_alive():
        return jnp.min(c_ref[:, 0:1, :]) < UNDERFLOW_LOG2

    def more(p):
        return (p < qi) & weights_alive()

    def body(p):
        i = 2 + 2 * p
        scores(block(i + 1), sb_ref)
        consume(block(i), sa_ref)

        @pl.when(weights_alive())
        def _():
            scores(block(i + 2), sa_ref)
            consume(block(i + 1), sb_ref)

        return p + 1

    if qi > 0:
        lax.while_loop(more, body, 0)

    for g in range(heads):
        o_ref[q0:q0 + tq, g * QK_DIM:(g + 1) * QK_DIM] = acc_ref[g].T.astype(o_ref.dtype)


def _sb_attn(u, *, batch, seq, tk, heads):
    t = u.shape[0]
    blk = 2 * tk
    nq = seq // blk
    ng = SB_HEADS // heads
    assert SB_HEADS % heads == 0 and seq % blk == 0
    w = heads * QK_DIM
    off = (4 * DIFF_HEADS * QK_DIM + DIFF_HEADS * DIFF_V_DIM) // w
    return pl.pallas_call(
        functools.partial(_sb_attn_kernel, tk=tk, heads=heads),
        grid=(batch, ng),
        in_specs=[
            pl.BlockSpec((seq, w), lambda b, h: (b, off + h)),
            pl.BlockSpec((seq, w), lambda b, h: (b, off + ng + h)),
            pl.BlockSpec((seq, w), lambda b, h: (b, off + 2 * ng + h)),
        ],
        out_specs=pl.BlockSpec((seq, w), lambda b, h: (b, h)),
        out_shape=jax.ShapeDtypeStruct((t, SB_HEADS * QK_DIM), BF16),
        scratch_shapes=[
            pltpu.VMEM((w, seq), BF16),
            pltpu.VMEM((tk + SUM_ROWS, 2 * tk), BF16),
            pltpu.VMEM((heads, tk, blk), F32),
            pltpu.VMEM((heads, tk, blk), F32),
            pltpu.VMEM((heads, C_ROWS, blk), F32),
            pltpu.VMEM((heads, QK_DIM, blk), F32),
        ],
        compiler_params=_params("parallel", "parallel"),
        name="sb_attn",
    )(u, u, u)


def _out_proj_kernel(x_ref, a_ref, b_ref, wa_ref, wb_ref, o_ref):
    o_ref[...] = x_ref[...] + _dot(a_ref[...], wa_ref[...]) + _dot(b_ref[...], wb_ref[...])


def _out_proj(x, a, b, w, *, tm):
    t, d = x.shape
    wa_rows, wb_rows = a.shape[1], b.shape[1]
    assert t % tm == 0 and wa_rows == wb_rows and w.shape[0] == wa_rows + wb_rows
    return pl.pallas_call(
        _out_proj_kernel,
        grid=(t // tm,),
        in_specs=[
            pl.BlockSpec((tm, d), lambda i: (i, 0)),
            pl.BlockSpec((tm, wa_rows), lambda i: (i, 0)),
            pl.BlockSpec((tm, wb_rows), lambda i: (i, 0)),
            pl.BlockSpec((wa_rows, d), lambda i: (0, 0), pipeline_mode=pl.Buffered(1)),
            pl.BlockSpec((wb_rows, d), lambda i: (1, 0), pipeline_mode=pl.Buffered(1)),
        ],
        out_specs=pl.BlockSpec((tm, d), lambda i: (i, 0)),
        out_shape=jax.ShapeDtypeStruct((t, d), F32),
        compiler_params=_params("parallel"),
        name="out_proj",
    )(x, a, b, w, w)


def _rope_tables(seq):
    half = ROPE_DIM // 2
    pos = jnp.arange(seq, dtype=F32)
    inv_freq = ROPE_THETA ** (-jnp.arange(0, ROPE_DIM, 2, dtype=F32) / ROPE_DIM)
    ang = pos[:, None] * inv_freq[None, :]
    cos, sin = jnp.cos(ang), jnp.sin(ang)
    ones = jnp.ones((seq, QK_DIM - ROPE_DIM), F32)
    zeros_h = jnp.zeros((seq, half), F32)
    zeros_r = jnp.zeros((seq, QK_DIM - ROPE_DIM), F32)
    cos_t = jnp.concatenate([cos, cos, ones], axis=1)
    sa_t = jnp.concatenate([-sin, zeros_h, zeros_r], axis=1)
    sb_t = jnp.concatenate([zeros_h, sin, zeros_r], axis=1)
    return cos_t, sa_t, sb_t


def _pick(n, pref):
    for c in pref:
        if n % c == 0:
            return c
    return n


def kernel(x, ffn1_norm, ffn1_w_gate, ffn1_w_up, ffn1_w_down, mix_norm, w_in, q_norm, k_norm,
           lambda_q1, lambda_k1, lambda_q2, lambda_k2, subln, w_out,
           ffn2_norm, ffn2_w_gate, ffn2_w_up, ffn2_w_down, final_norm):
    batch, seq, d = x.shape
    assert ffn1_norm.shape[0] == 1, "single-layer block"
    t = batch * seq
    l = 0
    dff = ffn1_w_gate.shape[-1]
    tm = _pick(t, (1024, 512, 256, 128))
    tf = _pick(dff, (512, 256, 128))
    tk = _pick(seq // 2, (256, 128))
    bf = lambda w: w[l].astype(BF16)
    mat = lambda w: w.reshape(w.shape[1:])
    row = lambda g: g[l].astype(F32)[None, :]

    xt = x.reshape(t, d)
    x1 = _ffn(xt, row(ffn1_norm), mat(ffn1_w_gate), mat(ffn1_w_up), mat(ffn1_w_down),
              row(final_norm), final_norm=False, tm=tm, tf=tf)

    cos_t, sa_t, sb_t = _rope_tables(seq)
    u = _in_proj(x1, row(mix_norm), bf(w_in), row(q_norm), row(k_norm), cos_t, sa_t, sb_t,
                 seq=seq, tm=_pick(seq, (1024, 512, 256, 128)), tn=1024)

    a = _diff_attn(u, row(lambda_q1), row(lambda_k1), row(lambda_q2), row(lambda_k2),
                   row(subln), batch=batch, seq=seq, tk=tk, heads=4)
    b = _sb_attn(u, batch=batch, seq=seq, tk=tk, heads=8)

    x2 = _out_proj(x1, a, b, bf(w_out), tm=tm)

    out = _ffn(x2, row(ffn2_norm), mat(ffn2_w_gate), mat(ffn2_w_up), mat(ffn2_w_down),
               row(final_norm), final_norm=True, tm=tm, tf=tf)
    return out.reshape(batch, seq, d)
```

```python
import functools
import math

import jax
import jax.numpy as jnp
from jax import lax
from jax.experimental import pallas as pl
from jax.experimental.pallas import tpu as pltpu

EPS = 1e-5
ROPE_THETA = 500000.0
LAMBDA_INIT = 0.8 - 0.6 * math.exp(-0.3 * 0)
LOG2E = math.log2(math.e)
MASKED_LOGIT = 1e30
UNDERFLOW_LOG2 = 160.0

DIFF_HEADS = 4
SB_HEADS = 8
QK_DIM = 128
DIFF_V_DIM = 2 * QK_DIM
ROPE_DIM = QK_DIM // 4

F32 = jnp.float32
BF16 = jnp.bfloat16

V7X_VMEM_BYTES = 64 * 1024 * 1024
VMEM_LIMIT_BYTES = V7X_VMEM_BYTES - 4 * 1024 * 1024
ROW_CHUNK = 256
SUM_ROWS = 16
C_ROWS = 8


def _params(*sem):
    return pltpu.CompilerParams(dimension_semantics=sem, vmem_limit_bytes=VMEM_LIMIT_BYTES)


def _rms(x, g):
    ms = jnp.mean(x * x, axis=-1, keepdims=True)
    return x * lax.rsqrt(ms + EPS) * g


def _dot(a, b):
    return jnp.dot(a, b, preferred_element_type=F32)


def _block_start(j, size):
    return j * size if isinstance(j, int) else pl.multiple_of(j * size, size)


def _dot_nt(a, b):
    return lax.dot_general(a, b, (((1,), (1,)), ((), ())), preferred_element_type=F32)


def _ffn_kernel(x_hbm, g_ref, wg_ref, wu_ref, wd_ref, fg_ref, o_ref, xn_ref, x_ref, x_sem,
                *, final_norm):
    i, f = pl.program_id(0), pl.program_id(1)
    tm = x_ref.shape[0]

    def x_copy(tile):
        return pltpu.make_async_copy(x_hbm.at[pl.ds(tile * tm, tm), :], x_ref, x_sem)

    @pl.when((i == 0) & (f == 0))
    def _():
        x_copy(0).start()

    def hidden_tile(xn, base):
        gate = _dot(xn, wg_ref[...].astype(BF16))
        up = _dot(xn, wu_ref[...].astype(BF16))
        h = (gate * jax.nn.sigmoid(gate)) * (up * 0.5)
        o_ref[...] = base + _dot(h.astype(BF16), wd_ref[...].astype(BF16))

    @pl.when(f == 0)
    def _():
        x_copy(i).wait()
        x = x_ref[...]
        xn = _rms(x, g_ref[...]).astype(BF16)
        xn_ref[...] = xn
        hidden_tile(xn, x)

    @pl.when((f == 1) & (i + 1 < pl.num_programs(0)))
    def _():
        x_copy(i + 1).start()

    @pl.when(f > 0)
    def _():
        hidden_tile(xn_ref[...], o_ref[...])

    if final_norm:
        @pl.when(f == pl.num_programs(1) - 1)
        def _():
            o_ref[...] = _rms(o_ref[...], fg_ref[...])


def _ffn(x, norm_g, wg, wu, wd, final_g, *, final_norm, tm, tf):
    t, d = x.shape
    dff = wg.shape[1]
    assert t % tm == 0 and dff % tf == 0 and dff // tf >= 2
    return pl.pallas_call(
        functools.partial(_ffn_kernel, final_norm=final_norm),
        grid=(t // tm, dff // tf),
        in_specs=[
            pl.BlockSpec(memory_space=pl.ANY),
            pl.BlockSpec((1, d), lambda i, f: (0, 0)),
            pl.BlockSpec((d, tf), lambda i, f: (0, f)),
            pl.BlockSpec((d, tf), lambda i, f: (0, f)),
            pl.BlockSpec((tf, d), lambda i, f: (f, 0)),
            pl.BlockSpec((1, d), lambda i, f: (0, 0)),
        ],
        out_specs=pl.BlockSpec((tm, d), lambda i, f: (i, 0)),
        out_shape=jax.ShapeDtypeStruct((t, d), F32),
        scratch_shapes=[pltpu.VMEM((tm, d), BF16), pltpu.VMEM((tm, d), F32),
                        pltpu.SemaphoreType.DMA(())],
        compiler_params=_params("arbitrary", "arbitrary"),
        name="ffn_final" if final_norm else "ffn",
    )(x, norm_g, wg, wu, wd, final_g)


def _in_proj_kernel(x_ref, g_ref, w_ref, qg_ref, kg_ref, cos_ref, sa_ref, sb_ref,
                    o_ref, xn_ref, *, tn, n_qk_tiles, sbq_lo, sbq_hi, scale):
    j = pl.program_id(1)

    tm = o_ref.shape[0]
    chunk = min(tm, ROW_CHUNK)

    def qk_tile(first):
        w = w_ref[:, pl.ds(pl.multiple_of(j * tn, tn), tn)]
        is_q = j < n_qk_tiles // 2
        gain = jnp.where(is_q, qg_ref[...], kg_ref[...])
        post = jnp.where(is_q, scale, 1.0).astype(F32)
        for r in range(tm // chunk):
            rows = slice(r * chunk, (r + 1) * chunk)
            if first:
                xn = _rms(x_ref[rows, :], g_ref[...]).astype(BF16)
                xn_ref[rows, :] = xn
            else:
                xn = xn_ref[rows, :]
            u = _dot(xn, w)
            cos, sa, sb = cos_ref[rows, :], sa_ref[rows, :], sb_ref[rows, :]
            for c in range(tn // QK_DIM):
                t = u[:, c * QK_DIM:(c + 1) * QK_DIM]
                ms = jnp.mean(t * t, axis=-1, keepdims=True)
                t = t * (lax.rsqrt(ms + EPS) * post) * gain
                t = (t * cos + pltpu.roll(t, QK_DIM - ROPE_DIM // 2, 1) * sa
                     + pltpu.roll(t, ROPE_DIM // 2, 1) * sb)
                o_ref[rows, c * QK_DIM:(c + 1) * QK_DIM] = t.astype(BF16)

    @pl.when(j == 0)
    def _():
        qk_tile(True)

    @pl.when((j > 0) & (j < n_qk_tiles))
    def _():
        qk_tile(False)

    @pl.when(j >= n_qk_tiles)
    def _():
        post = jnp.where((j >= sbq_lo) & (j < sbq_hi), scale, 1.0).astype(F32)
        u = _dot(xn_ref[...], w_ref[:, pl.ds(pl.multiple_of(j * tn, tn), tn)])
        o_ref[...] = (u * post).astype(BF16)


def _in_proj(x, norm_g, w, qg, kg, cos_t, sa_t, sb_t, *, seq, tm, tn):
    t, d = x.shape
    n = w.shape[1]
    diff_qk_cols = DIFF_HEADS * QK_DIM
    assert t % tm == 0 and seq % tm == 0 and n % tn == 0 and (2 * diff_qk_cols) % tn == 0
    assert (DIFF_HEADS * DIFF_V_DIM) % tn == 0 and (SB_HEADS * QK_DIM) % tn == 0
    n_qk_tiles = 4 * diff_qk_cols // tn
    sbq_lo = (4 * diff_qk_cols + DIFF_HEADS * DIFF_V_DIM) // tn
    sbq_hi = sbq_lo + SB_HEADS * QK_DIM // tn
    s_blocks = seq // tm
    rope_spec = pl.BlockSpec((tm, QK_DIM), lambda i, j: (i % s_blocks, 0))
    kern = functools.partial(_in_proj_kernel, tn=tn, n_qk_tiles=n_qk_tiles,
                             sbq_lo=sbq_lo, sbq_hi=sbq_hi, scale=QK_DIM ** -0.5 * LOG2E)
    return pl.pallas_call(
        kern,
        grid=(t // tm, n // tn),
        in_specs=[
            pl.BlockSpec((tm, d), lambda i, j: (i, 0)),
            pl.BlockSpec((1, d), lambda i, j: (0, 0)),
            pl.BlockSpec((d, n), lambda i, j: (0, 0), pipeline_mode=pl.Buffered(1)),
            pl.BlockSpec((1, QK_DIM), lambda i, j: (0, 0)),
            pl.BlockSpec((1, QK_DIM), lambda i, j: (0, 0)),
            rope_spec, rope_spec, rope_spec,
        ],
        out_specs=pl.BlockSpec((tm, tn), lambda i, j: (i, j)),
        out_shape=jax.ShapeDtypeStruct((t, n), BF16),
        scratch_shapes=[pltpu.VMEM((tm, d), BF16)],
        compiler_params=_params("parallel", "arbitrary"),
        name="in_proj",
    )(x, norm_g, w, qg, kg, cos_t, sa_t, sb_t)


def _diff_attn_kernel(q1_ref, q2_ref, k1_ref, k2_ref, v_ref, lq1_ref, lk1_ref, lq2_ref,
                      lk2_ref, sub_ref, o_ref, vt_ref, sa_ref, sb_ref, m_ref, l_ref, acc_ref,
                      *, tk, heads):
    tq = 2 * tk
    seq = q1_ref.shape[0]
    vt_ref[...] = v_ref[...].astype(F32).T.astype(BF16)

    q_refs = (q1_ref, q2_ref)
    k_refs = (k1_ref, k2_ref)
    krow = lax.broadcasted_iota(jnp.int32, (tk, tq), 0)
    kcol = lax.broadcasted_iota(jnp.int32, (tk, tq), 1)
    triangle = (lax.broadcasted_iota(jnp.int32, (tk, tk), 0)
                <= lax.broadcasted_iota(jnp.int32, (tk, tk), 1))
    lam = (jnp.exp(jnp.sum(lq1_ref[...] * lk1_ref[...], axis=-1, keepdims=True))
           - jnp.exp(jnp.sum(lq2_ref[...] * lk2_ref[...], axis=-1, keepdims=True))
           + LAMBDA_INIT)

    chains = [(g, mp) for g in range(heads) for mp in range(2)]

    every_query = slice(0, tq)
    late_queries = slice(tk, tq)

    for qi in range(seq // tq):
        _diff_query_block(qi, q_refs, k_refs, vt_ref, sub_ref, o_ref, sa_ref, sb_ref, m_ref,
                          l_ref, acc_ref, chains, krow, kcol, triangle, lam, every_query,
                          late_queries, tk=tk, heads=heads)


def _diff_query_block(qi, q_refs, k_refs, vt_ref, sub_ref, o_ref, sa_ref, sb_ref, m_ref, l_ref,
                      acc_ref, chains, krow, kcol, triangle, lam, every_query, late_queries,
                      *, tk, heads):
    tq = 2 * tk
    q0 = qi * tq

    def scores(j, s_ref, qs=every_query):
        start = _block_start(j, tk)
        width = qs.stop - qs.start
        rows = slice(q0 + qs.start, q0 + qs.stop)
        for idx, (g, mp) in enumerate(chains):
            hd = slice(g * QK_DIM, (g + 1) * QK_DIM)
            s_ref[idx, :, :width] = _dot_nt(k_refs[mp][pl.ds(start, tk), hd], q_refs[mp][rows, hd])

    def consume(j, s_ref, visible=None, qs=every_query):
        start = _block_start(j, tk)
        width = qs.stop - qs.start
        probs, alphas = [], []
        for idx in range(len(chains)):
            s = s_ref[idx, :, :width]
            if visible is not None:
                s = jnp.where(visible, s, -jnp.inf)
            m_old = m_ref[idx, :, qs]
            m = jnp.maximum(m_old, jnp.max(s, axis=0, keepdims=True))
            alpha = jnp.exp2(m_old - m)
            p = jnp.exp2(s - m)
            l_ref[idx, :, qs] = alpha * l_ref[idx, :, qs] + jnp.sum(p, axis=0, keepdims=True)
            m_ref[idx, :, qs] = m
            alphas.append(alpha)
            probs.append(p.astype(BF16))
        for idx, (g, mp) in enumerate(chains):
            vt = vt_ref[g * DIFF_V_DIM:(g + 1) * DIFF_V_DIM, pl.ds(start, tk)]
            acc_ref[idx, :, qs] = alphas[idx] * acc_ref[idx, :, qs] + _dot(vt, probs[idx])

    m_ref[...] = jnp.full(m_ref.shape, -jnp.inf, F32)
    l_ref[...] = jnp.zeros(l_ref.shape, F32)
    acc_ref[...] = jnp.zeros(acc_ref.shape, F32)

    scores(0, sa_ref)

    def body(p, carry):
        j = 2 * p
        scores(j + 1, sb_ref)
        consume(j, sa_ref)
        scores(j + 2, sa_ref)
        consume(j + 1, sb_ref)
        return carry

    if qi > 0:
        lax.fori_loop(0, qi, body, 0)
    scores(2 * qi + 1, sb_ref, late_queries)
    consume(2 * qi, sa_ref, krow <= kcol)
    consume(2 * qi + 1, sb_ref, triangle, late_queries)

    for g in range(heads):
        o = (acc_ref[2 * g] * (1.0 / l_ref[2 * g])
             - acc_ref[2 * g + 1] * (lam / l_ref[2 * g + 1]))
        ms = jnp.mean(o * o, axis=0, keepdims=True)
        o = (o * lax.rsqrt(ms + EPS)).T * sub_ref[...] * (1.0 - LAMBDA_INIT)
        o_ref[q0:q0 + tq, g * DIFF_V_DIM:(g + 1) * DIFF_V_DIM] = o.astype(o_ref.dtype)


def _diff_attn(u, lq1, lk1, lq2, lk2, subln, *, batch, seq, tk, heads):
    t = u.shape[0]
    blk = 2 * tk
    nq = seq // blk
    ng = DIFF_HEADS // heads
    assert DIFF_HEADS % heads == 0 and seq % blk == 0
    v_off = 4 * DIFF_HEADS * QK_DIM // (heads * DIFF_V_DIM)
    lam_spec = pl.BlockSpec((1, QK_DIM), lambda b, h: (0, 0))
    return pl.pallas_call(
        functools.partial(_diff_attn_kernel, tk=tk, heads=heads),
        grid=(batch, ng),
        in_specs=[
            pl.BlockSpec((seq, heads * QK_DIM), lambda b, h: (b, h)),
            pl.BlockSpec((seq, heads * QK_DIM), lambda b, h: (b, ng + h)),
            pl.BlockSpec((seq, heads * QK_DIM), lambda b, h: (b, 2 * ng + h)),
            pl.BlockSpec((seq, heads * QK_DIM), lambda b, h: (b, 3 * ng + h)),
            pl.BlockSpec((seq, heads * DIFF_V_DIM), lambda b, h: (b, v_off + h)),
            lam_spec, lam_spec, lam_spec, lam_spec,
            pl.BlockSpec((1, DIFF_V_DIM), lambda b, h: (0, 0)),
        ],
        out_specs=pl.BlockSpec((seq, heads * DIFF_V_DIM), lambda b, h: (b, h)),
        out_shape=jax.ShapeDtypeStruct((t, DIFF_HEADS * DIFF_V_DIM), BF16),
        scratch_shapes=[
            pltpu.VMEM((heads * DIFF_V_DIM, seq), BF16),
            pltpu.VMEM((2 * heads, tk, blk), F32),
            pltpu.VMEM((2 * heads, tk, blk), F32),
            pltpu.VMEM((2 * heads, 1, blk), F32),
            pltpu.VMEM((2 * heads, 1, blk), F32),
            pltpu.VMEM((2 * heads, DIFF_V_DIM, blk), F32),
        ],
        compiler_params=_params("parallel", "parallel"),
        name="diff_attn",
    )(u, u, u, u, u, lq1, lk1, lq2, lk2, subln)


def _sb_attn_kernel(q_ref, k_ref, v_ref, o_ref, vt_ref, up_ref, sa_ref, sb_ref, c_ref, acc_ref,
                    *, tk, heads):
    qi = pl.program_id(2)
    tq = 2 * tk
    krow = lax.broadcasted_iota(jnp.int32, (tk, tq), 0)
    kcol = lax.broadcasted_iota(jnp.int32, (tk, tq), 1)

    @pl.when(qi == 0)
    def _():
        vt_ref[...] = v_ref[...].astype(F32).T.astype(BF16)
        r = lax.broadcasted_iota(jnp.int32, (tk, tk), 0)
        c = lax.broadcasted_iota(jnp.int32, (tk, tk), 1)
        upper = (c >= r).astype(BF16)
        upper = jnp.concatenate([upper, upper], axis=1)
        up_ref[...] = jnp.concatenate([upper, jnp.ones((SUM_ROWS, 2 * tk), BF16)], axis=0)

    heads_hd = [slice(g * QK_DIM, (g + 1) * QK_DIM) for g in range(heads)]
    every_query = slice(0, tq)
    late_queries = slice(tk, tq)

    def scores(j, s_ref, qs=every_query):
        start = pl.multiple_of(j * tk, tk)
        width = qs.stop - qs.start
        for g, hd in enumerate(heads_hd):
            s_ref[g, :, :width] = _dot_nt(k_ref[pl.ds(start, tk), hd], q_ref[qs, hd])

    def consume(j, s_ref, visible=None, qs=every_query):
        start = pl.multiple_of(j * tk, tk)
        width = qs.stop - qs.start
        logits, hilo = [], []
        for g in range(heads):
            z = s_ref[g, :, :width]
            if visible is not None:
                z = jnp.where(visible, z, -MASKED_LOGIT)
            sp = jnp.maximum(z, 0.0) + jnp.log2(1.0 + jnp.exp2(-jnp.abs(z)))
            logits.append(z - c_ref[g, 0:1, qs])
            hi = sp.astype(BF16)
            lo = (sp - hi.astype(F32)).astype(BF16)
            hilo.append(jnp.concatenate([hi, lo], axis=0))
        sums = [_dot(up_ref[...], x) for x in hilo]
        weights = []
        for g in range(heads):
            weights.append(jnp.exp2(logits[g] - sums[g][:tk]).astype(BF16))
            c_ref[g, :, qs] += sums[g][tk:tk + C_ROWS]
        for g, hd in enumerate(heads_hd):
            acc_ref[g, :, qs] += _dot(vt_ref[hd, pl.ds(start, tk)], weights[g])

    c_ref[...] = jnp.zeros(c_ref.shape, F32)
    acc_ref[...] = jnp.zeros(acc_ref.shape, F32)

    last = 2 * qi + 1

    def block(i):
        return jnp.maximum(last - i, 0)

    triangle = (lax.broadcasted_iota(jnp.int32, (tk, tk), 0)
                < lax.broadcasted_iota(jnp.int32, (tk, tk), 1))
    scores(last, sa_ref, late_queries)
    scores(block(1), sb_ref)
    consume(last, sa_ref, triangle, late_queries)
    scores(block(2), sa_ref)
    consume(block(1), sb_ref, krow < kcol)

    def weights_alive():
        return jnp.min(c_ref[:, 0:1, :]) < UNDERFLOW_LOG2

    def more(p):
        return (p < qi) & weights_alive()

    def body(p):
        i = 2 + 2 * p
        scores(block(i + 1), sb_ref)
        consume(block(i), sa_ref)

        @pl.when(weights_alive())
        def _():
            scores(block(i + 2), sa_ref)
            consume(block(i + 1), sb_ref)

        return p + 1

    lax.while_loop(more, body, 0)

    for g in range(heads):
        o_ref[:, g * QK_DIM:(g + 1) * QK_DIM] = acc_ref[g].T.astype(o_ref.dtype)


def _sb_attn(u, *, batch, seq, tk, heads):
    t = u.shape[0]
    blk = 2 * tk
    nq = seq // blk
    ng = SB_HEADS // heads
    assert SB_HEADS % heads == 0 and seq % blk == 0
    w = heads * QK_DIM
    off = (4 * DIFF_HEADS * QK_DIM + DIFF_HEADS * DIFF_V_DIM) // w
    return pl.pallas_call(
        functools.partial(_sb_attn_kernel, tk=tk, heads=heads),
        grid=(batch, ng, nq),
        in_specs=[
            pl.BlockSpec((blk, w), lambda b, h, q: (b * nq + q, off + h)),
            pl.BlockSpec((seq, w), lambda b, h, q: (b, off + ng + h)),
            pl.BlockSpec((seq, w), lambda b, h, q: (b, off + 2 * ng + h)),
        ],
        out_specs=pl.BlockSpec((blk, w), lambda b, h, q: (b * nq + q, h)),
        out_shape=jax.ShapeDtypeStruct((t, SB_HEADS * QK_DIM), BF16),
        scratch_shapes=[
            pltpu.VMEM((w, seq), BF16),
            pltpu.VMEM((tk + SUM_ROWS, 2 * tk), BF16),
            pltpu.VMEM((heads, tk, blk), F32),
            pltpu.VMEM((heads, tk, blk), F32),
            pltpu.VMEM((heads, C_ROWS, blk), F32),
            pltpu.VMEM((heads, QK_DIM, blk), F32),
        ],
        compiler_params=_params("parallel", "parallel", "arbitrary"),
        name="sb_attn",
    )(u, u, u)


def _out_proj_kernel(x_ref, a_ref, b_ref, wa_ref, wb_ref, o_ref):
    o_ref[...] = x_ref[...] + _dot(a_ref[...], wa_ref[...]) + _dot(b_ref[...], wb_ref[...])


def _out_proj(x, a, b, w, *, tm):
    t, d = x.shape
    wa_rows, wb_rows = a.shape[1], b.shape[1]
    assert t % tm == 0 and wa_rows == wb_rows and w.shape[0] == wa_rows + wb_rows
    return pl.pallas_call(
        _out_proj_kernel,
        grid=(t // tm,),
        in_specs=[
            pl.BlockSpec((tm, d), lambda i: (i, 0)),
            pl.BlockSpec((tm, wa_rows), lambda i: (i, 0)),
            pl.BlockSpec((tm, wb_rows), lambda i: (i, 0)),
            pl.BlockSpec((wa_rows, d), lambda i: (0, 0), pipeline_mode=pl.Buffered(1)),
            pl.BlockSpec((wb_rows, d), lambda i: (1, 0), pipeline_mode=pl.Buffered(1)),
        ],
        out_specs=pl.BlockSpec((tm, d), lambda i: (i, 0)),
        out_shape=jax.ShapeDtypeStruct((t, d), F32),
        compiler_params=_params("parallel"),
        name="out_proj",
    )(x, a, b, w, w)


def _rope_tables(seq):
    half = ROPE_DIM // 2
    pos = jnp.arange(seq, dtype=F32)
    inv_freq = ROPE_THETA ** (-jnp.arange(0, ROPE_DIM, 2, dtype=F32) / ROPE_DIM)
    ang = pos[:, None] * inv_freq[None, :]
    cos, sin = jnp.cos(ang), jnp.sin(ang)
    ones = jnp.ones((seq, QK_DIM - ROPE_DIM), F32)
    zeros_h = jnp.zeros((seq, half), F32)
    zeros_r = jnp.zeros((seq, QK_DIM - ROPE_DIM), F32)
    cos_t = jnp.concatenate([cos, cos, ones], axis=1)
    sa_t = jnp.concatenate([-sin, zeros_h, zeros_r], axis=1)
    sb_t = jnp.concatenate([zeros_h, sin, zeros_r], axis=1)
    return cos_t, sa_t, sb_t


def _pick(n, pref):
    for c in pref:
        if n % c == 0:
            return c
    return n


def kernel(x, ffn1_norm, ffn1_w_gate, ffn1_w_up, ffn1_w_down, mix_norm, w_in, q_norm, k_norm,
           lambda_q1, lambda_k1, lambda_q2, lambda_k2, subln, w_out,
           ffn2_norm, ffn2_w_gate, ffn2_w_up, ffn2_w_down, final_norm):
    batch, seq, d = x.shape
    assert ffn1_norm.shape[0] == 1, "single-layer block"
    t = batch * seq
    l = 0
    dff = ffn1_w_gate.shape[-1]
    tm = _pick(t, (1024, 512, 256, 128))
    tf = _pick(dff, (512, 256, 128))
    tk = _pick(seq // 2, (256, 128))
    bf = lambda w: w[l].astype(BF16)
    mat = lambda w: w.reshape(w.shape[1:])
    row = lambda g: g[l].astype(F32)[None, :]

    xt = x.reshape(t, d)
    x1 = _ffn(xt, row(ffn1_norm), mat(ffn1_w_gate), mat(ffn1_w_up), mat(ffn1_w_down),
              row(final_norm), final_norm=False, tm=tm, tf=tf)

    cos_t, sa_t, sb_t = _rope_tables(seq)
    u = _in_proj(x1, row(mix_norm), bf(w_in), row(q_norm), row(k_norm), cos_t, sa_t, sb_t,
                 seq=seq, tm=_pick(seq, (1024, 512, 256, 128)), tn=1024)

    a = _diff_attn(u, row(lambda_q1), row(lambda_k1), row(lambda_q2), row(lambda_k2),
                   row(subln), batch=batch, seq=seq, tk=tk, heads=4)
    b = _sb_attn(u, batch=batch, seq=seq, tk=tk, heads=8)

    x2 = _out_proj(x1, a, b, bf(w_out), tm=tm)

    out = _ffn(x2, row(ffn2_norm), mat(ffn2_w_gate), mat(ffn2_w_up), mat(ffn2_w_down),
               row(final_norm), final_norm=True, tm=tm, tf=tf)
    return out.reshape(batch, seq, d)
```
